```python
import jax, jax.numpy as jnp
from jax import lax
import numpy as np

D_MODEL = 1024
BATCH = 32
SEQ = 256
DEPTH = 2
DEC_BATCH = 8
DEC_SEQ = 4096
PAST_LEN = 256

GRID_W = 64
N_EVEN = (DEPTH + 1) // 2
N_ODD = DEPTH // 2
EPS = 1e-6
H_MLA = 8
Q_RANK = 256
KV_RANK = 128
MLA_NOPE = 64
ROPE_DIM = 32
MLA_QK = MLA_NOPE + ROPE_DIM
MLA_V = 64
ROPE_BASE = 10000.0
Q_BLOCK = 128
H_RET = 4
RET_DK = 64
RET_DV = 128
RET_CHUNK = 128
EVEN_IN = Q_RANK + KV_RANK + ROPE_DIM + 2 * H_RET * RET_DK + 2 * H_RET * RET_DV
EVEN_MIX = H_MLA * MLA_V + H_RET * RET_DV
CONV_DIM = 1024
CONV_W = 31
N_GROUPS = 4
EXP_PER_GROUP = 8
N_EXPERTS = N_GROUPS * EXP_PER_GROUP
TOP_K = 2
D_EXPERT = 256
EXPERT_BLOCK = 128

kernel_name = 'hybrid_mla_retention_conformer_hmoe_dit_step'


def rms_norm(x, g):
    xf = x.astype(jnp.float32)
    y = xf * lax.rsqrt(jnp.mean(xf * xf, axis=-1, keepdims=True) + EPS)
    return (y * g.astype(jnp.float32)).astype(x.dtype)


def layer_norm(x, g, b):
    xf = x.astype(jnp.float32)
    mu = jnp.mean(xf, axis=-1, keepdims=True)
    var = jnp.mean(jnp.square(xf - mu), axis=-1, keepdims=True)
    y = (xf - mu) * lax.rsqrt(var + EPS)
    return (y * g.astype(jnp.float32) + b.astype(jnp.float32)).astype(x.dtype)


def modulation(cond, w, b):
    m = jax.nn.silu(cond) @ w + b
    return jnp.split(m[:, None, :], 6, axis=-1)


def modulate(h, shift, scale):
    return h * (1.0 + scale) + shift


def axial_rope(L):
    rows = L // GRID_W
    r = jnp.repeat(jnp.arange(rows, dtype=jnp.float32), GRID_W)
    col = jnp.tile(jnp.arange(GRID_W, dtype=jnp.float32), rows)
    n_f = ROPE_DIM // 4
    freqs = ROPE_BASE ** (-jnp.arange(n_f, dtype=jnp.float32) / n_f)
    ang = jnp.concatenate([r[:, None] * freqs, col[:, None] * freqs], axis=-1)
    return jnp.cos(ang), jnp.sin(ang)


def apply_rope(x, rope):
    cos, sin = rope
    half = ROPE_DIM // 2
    c = cos[None, :, None, :].astype(x.dtype)
    s = sin[None, :, None, :].astype(x.dtype)
    x1, x2 = x[..., :half], x[..., half:]
    return jnp.concatenate([x1 * c - x2 * s, x1 * s + x2 * c], axis=-1)


def split_even(z):
    sizes = (Q_RANK, KV_RANK, ROPE_DIM, H_RET * RET_DK, H_RET * RET_DK, H_RET * RET_DV, H_RET * RET_DV)
    idx, acc = [], 0
    for s in sizes[:-1]:
        acc += s
        idx.append(acc)
    return jnp.split(z, idx, axis=-1)


def mla_queries(cq, q_a_norm, w_q_up, q_norm, rope):
    B, L = cq.shape[:2]
    q = (rms_norm(cq, q_a_norm) @ w_q_up).reshape(B, L, H_MLA, MLA_QK)
    q = rms_norm(q, q_norm)
    if rope is not None:
        q = jnp.concatenate([q[..., :MLA_NOPE], apply_rope(q[..., MLA_NOPE:], rope)], axis=-1)
    return q


def mla_keys(ckv, k_rope, w_kv_up, k_norm, rope):
    B, L = ckv.shape[:2]
    kv = (ckv @ w_kv_up).reshape(B, L, H_MLA, MLA_NOPE + MLA_V)
    k_nope, v = kv[..., :MLA_NOPE], kv[..., MLA_NOPE:]
    k_pe = jnp.broadcast_to(k_rope[:, :, None, :], (B, L, H_MLA, ROPE_DIM)).astype(k_nope.dtype)
    k = rms_norm(jnp.concatenate([k_nope, k_pe], axis=-1), k_norm)
    if rope is not None:
        k = jnp.concatenate([k[..., :MLA_NOPE], apply_rope(k[..., MLA_NOPE:], rope)], axis=-1)
    return k, v


def block_attention(q, k, v):
    B, Lq, H, dq = q.shape
    nb = Lq // Q_BLOCK
    qb = q.reshape(B, nb, Q_BLOCK, H, dq).transpose(1, 0, 2, 3, 4)
    scale = dq ** -0.5

    def one(qblk):
        s = jnp.einsum('bqhd,bkhd->bhqk', qblk, k, preferred_element_type=jnp.float32) * scale
        p = jax.nn.softmax(s, axis=-1).astype(v.dtype)
        return jnp.einsum('bhqk,bkhd->bqhd', p, v)

    o = lax.map(one, qb)
    return o.transpose(1, 0, 2, 3, 4).reshape(B, Lq, H, v.shape[-1])


def retention_chunkwise(q, k, v, log_g, s0):
    B, L, H, DK = q.shape
    DV = v.shape[-1]
    C = RET_CHUNK
    nc = L // C
    qc = q.reshape(B, nc, C, H, DK)
    kc = k.reshape(B, nc, C, H, DK)
    vc = v.reshape(B, nc, C, H, DV)
    pos = jnp.arange(C, dtype=jnp.float32)
    diff = pos[:, None] - pos[None, :]
    causal = diff >= 0
    dmask = jnp.where(causal[..., None], jnp.exp(jnp.where(causal, diff, 0.0)[..., None] * log_g), 0.0)
    scores = jnp.einsum('bnqhd,bnkhd->bnhqk', qc, kc) * dmask.transpose(2, 0, 1)
    inner = jnp.einsum('bnhqk,bnkhe->bnqhe', scores, vc)
    w_k = jnp.exp((C - 1 - pos)[:, None] * log_g)
    u = jnp.einsum('bnkhd,kh,bnkhe->bnhde', kc, w_k, vc)
    chunk_decay = jnp.exp(C * log_g)[None, :, None, None]

    def step(s, u_j):
        return chunk_decay * s + u_j, s

    s_final, s_prev = lax.scan(step, s0.astype(jnp.float32), jnp.moveaxis(u, 1, 0))
    xi = jnp.exp((pos + 1.0)[:, None] * log_g)
    cross = jnp.einsum('bnqhd,qh,nbhde->bnqhe', qc, xi, s_prev)
    return (inner + cross).reshape(B, L, H, DV), s_final


def retention_bidir(rq, rk, rv, decay_logit, s0):
    B, L = rq.shape[:2]
    q = rq.reshape(B, L, H_RET, RET_DK).astype(jnp.float32)
    k = rk.reshape(B, L, H_RET, RET_DK).astype(jnp.float32) * (RET_DK ** -0.5)
    v = rv.reshape(B, L, H_RET, RET_DV).astype(jnp.float32)
    log_g = jax.nn.log_sigmoid(decay_logit.astype(jnp.float32))
    o_f, s_f = retention_chunkwise(q, k, v, log_g[0], s0[:, 0])
    o_b, s_b = retention_chunkwise(q[:, ::-1], k[:, ::-1], v[:, ::-1], log_g[1], s0[:, 1])
    return o_f + o_b[:, ::-1], jnp.stack([s_f, s_b], axis=1)


def retention_out(o, gn_gain, gate):
    B, L = o.shape[:2]
    mu = jnp.mean(o, axis=-1, keepdims=True)
    var = jnp.mean(jnp.square(o - mu), axis=-1, keepdims=True)
    o = ((o - mu) * lax.rsqrt(var + EPS)).reshape(B, L, H_RET * RET_DV) * gn_gain.astype(jnp.float32)
    return jax.nn.silu(gate) * o.astype(gate.dtype)


def even_outputs(q, k, v, rq, rk, rv, rg, decay_logit, s0, gn_gain, w_out):
    B, L = q.shape[:2]
    o_mla = block_attention(q, k, v).reshape(B, L, H_MLA * MLA_V)
    o_ret, s_fin = retention_bidir(rq, rk, rv, decay_logit, s0)
    o_ret = retention_out(o_ret, gn_gain, rg)
    y = jnp.concatenate([o_mla, o_ret.astype(o_mla.dtype)], axis=-1) @ w_out
    return y, s_fin


def even_mixer_context(h, w_in, q_a_norm, w_q_up, kv_a_norm, w_kv_up, q_norm, k_norm, decay_logit, gn_gain, w_out):
    cq, ckv, kr, rq, rk, rv, rg = split_even(h @ w_in)
    ckv = rms_norm(ckv, kv_a_norm)
    q = mla_queries(cq, q_a_norm, w_q_up, q_norm, None)
    k, v = mla_keys(ckv, kr, w_kv_up, k_norm, None)
    s0 = jnp.zeros((h.shape[0], 2, H_RET, RET_DK, RET_DV), jnp.float32)
    y, s_fin = even_outputs(q, k, v, rq, rk, rv, rg, decay_logit, s0, gn_gain, w_out)
    return y, ckv, kr, s_fin


def even_mixer_latent(h, cache_ckv, cache_kr, state, w_in, q_a_norm, w_q_up, kv_a_norm, w_kv_up, q_norm, k_norm, decay_logit, gn_gain, w_out):
    L = h.shape[1]
    rope = axial_rope(L)
    cq, ckv, kr, rq, rk, rv, rg = split_even(h @ w_in)
    ckv = rms_norm(ckv, kv_a_norm)
    q = mla_queries(cq, q_a_norm, w_q_up, q_norm, rope)
    k_lat, v_lat = mla_keys(ckv, kr, w_kv_up, k_norm, rope)
    k_ctx, v_ctx = mla_keys(cache_ckv, cache_kr, w_kv_up, k_norm, None)
    k = jnp.concatenate([k_ctx.astype(k_lat.dtype), k_lat], axis=1)
    v = jnp.concatenate([v_ctx.astype(v_lat.dtype), v_lat], axis=1)
    y, _ = even_outputs(q, k, v, rq, rk, rv, rg, decay_logit, state, gn_gain, w_out)
    return y


def depthwise_conv(u, w, b):
    out = lax.conv_general_dilated(u, w[:, None, :].astype(u.dtype), window_strides=(1,),
                                   padding=[(CONV_W // 2, CONV_W // 2)],
                                   dimension_numbers=('NWC', 'WIO', 'NWC'),
                                   feature_group_count=u.shape[-1])
    return out + b


def conformer_conv(h, w_in, dw, dw_b, ln_g, ln_b, w_out):
    a, g = jnp.split(h @ w_in, 2, axis=-1)
    u = a * jax.nn.sigmoid(g)
    u = depthwise_conv(u, dw, dw_b)
    u = layer_norm(u, ln_g, ln_b)
    return jax.nn.silu(u) @ w_out


def grouped_expert_mlp(xf, eid, top_w, w_up, w_down):
    N, D = xf.shape
    M = N * TOP_K
    e_flat = eid.reshape(M)
    tok = (jnp.arange(M, dtype=jnp.int32) // TOP_K).astype(jnp.int32)
    order = jnp.argsort(e_flat)
    e_sorted = e_flat[order]
    counts = jnp.bincount(e_flat, length=N_EXPERTS).astype(jnp.int32)
    padded = (counts + EXPERT_BLOCK - 1) // EXPERT_BLOCK * EXPERT_BLOCK
    pad_end = jnp.cumsum(padded)
    pad_start = pad_end - padded
    start = jnp.cumsum(counts) - counts
    dest_sorted = (pad_start[e_sorted] + (jnp.arange(M, dtype=jnp.int32) - start[e_sorted])).astype(jnp.int32)
    n_blocks = -(-M // EXPERT_BLOCK) + N_EXPERTS
    P = n_blocks * EXPERT_BLOCK
    buf_tok = jnp.full((P,), N, jnp.int32).at[dest_sorted].set(tok[order])
    x_pad = jnp.concatenate([xf, jnp.zeros((1, D), xf.dtype)], axis=0)
    x_buf = x_pad[buf_tok].reshape(n_blocks, EXPERT_BLOCK, D)
    blk_start = jnp.arange(n_blocks, dtype=jnp.int32) * EXPERT_BLOCK
    blk_exp = jnp.minimum(jnp.searchsorted(pad_end, blk_start, side='right'), N_EXPERTS - 1)

    def run(args):
        xb, e = args
        a, b = jnp.split(xb @ w_up[e], 2, axis=-1)
        return (jax.nn.silu(a) * b) @ w_down[e]

    out_buf = lax.map(run, (x_buf, blk_exp)).reshape(P, D)
    dest = jnp.zeros((M,), jnp.int32).at[order].set(dest_sorted)
    y = out_buf[dest].reshape(N, TOP_K, D) * top_w[..., None].astype(xf.dtype)
    return jnp.sum(y, axis=1)


def hier_moe(h, w_group, b_group, w_expert, b_expert, w_up, w_down):
    B, L, D = h.shape
    N = B * L
    xf = h.reshape(N, D)
    rows = jnp.arange(N)
    p_group = jax.nn.softmax((xf @ w_group).astype(jnp.float32) + b_group.astype(jnp.float32), axis=-1)
    g_idx = jnp.argmax(p_group, axis=-1)
    g_w = p_group[rows, g_idx]
    e_logits = ((xf @ w_expert).astype(jnp.float32) + b_expert.astype(jnp.float32)).reshape(N, N_GROUPS, EXP_PER_GROUP)[rows, g_idx]
    p_exp = jax.nn.softmax(e_logits, axis=-1)
    top_w, top_i = lax.top_k(p_exp, TOP_K)
    top_w = top_w / jnp.sum(top_w, axis=-1, keepdims=True) * g_w[:, None]
    eid = (g_idx[:, None] * EXP_PER_GROUP + top_i).astype(jnp.int32)
    return grouped_expert_mlp(xf, eid, top_w, w_up, w_down).reshape(B, L, D)


def setup_inputs(seed: int = 0) -> dict:
    key = jax.random.key(seed)
    ks = iter(jax.random.split(key, 48))

    def nrm(shape, scale=1.0):
        return jax.random.normal(next(ks), shape, jnp.float32) * scale

    def gain(shape):
        return 1.0 + nrm(shape, 0.02)

    a = 5.0 + jnp.arange(H_RET, dtype=jnp.float32)
    decay = jnp.log(2.0 ** a - 1.0)[None, None, :] + nrm((N_EVEN, 2, H_RET), 0.01)
    return {
        'x_prompt': nrm((BATCH, SEQ, D_MODEL)),
        'x_sample': nrm((DEC_BATCH, DEC_SEQ, D_MODEL)),
        'cache_mla_ckv': nrm((DEC_BATCH, N_EVEN, PAST_LEN, KV_RANK)),
        'cache_mla_krope': nrm((DEC_BATCH, N_EVEN, PAST_LEN, ROPE_DIM), 0.5),
        'state_retention': nrm((DEC_BATCH, N_EVEN, 2, H_RET, RET_DK, RET_DV), 0.5),
        'c': nrm((DEC_BATCH, D_MODEL)),
        'c_ctx': nrm((D_MODEL,)),
        'ada_w': nrm((DEPTH, D_MODEL, 6 * D_MODEL), 0.5 * D_MODEL ** -0.5),
        'ada_b': nrm((DEPTH, 6 * D_MODEL), 0.02),
        'norm_mix': gain((DEPTH, D_MODEL)),
        'norm_ffn': gain((DEPTH, D_MODEL)),
        'ev_w_in': nrm((N_EVEN, D_MODEL, EVEN_IN), D_MODEL ** -0.5),
        'ev_q_a_norm': gain((N_EVEN, Q_RANK)),
        'ev_w_q_up': nrm((N_EVEN, Q_RANK, H_MLA * MLA_QK), Q_RANK ** -0.5),
        'ev_kv_a_norm': gain((N_EVEN, KV_RANK)),
        'ev_w_kv_up': nrm((N_EVEN, KV_RANK, H_MLA * (MLA_NOPE + MLA_V)), KV_RANK ** -0.5),
        'ev_q_norm': gain((N_EVEN, MLA_QK)),
        'ev_k_norm': gain((N_EVEN, MLA_QK)),
        'ev_ret_decay': decay,
        'ev_ret_gn': gain((N_EVEN, H_RET * RET_DV)),
        'ev_w_out': nrm((N_EVEN, EVEN_MIX, D_MODEL), EVEN_MIX ** -0.5),
        'od_w_in': nrm((N_ODD, D_MODEL, 2 * CONV_DIM), D_MODEL ** -0.5),
        'od_dw': nrm((N_ODD, CONV_W, CONV_DIM), CONV_W ** -0.5),
        'od_dw_b': nrm((N_ODD, CONV_DIM), 0.02),
        'od_ln_g': gain((N_ODD, CONV_DIM)),
        'od_ln_b': nrm((N_ODD, CONV_DIM), 0.02),
        'od_w_out': nrm((N_ODD, CONV_DIM, D_MODEL), CONV_DIM ** -0.5),
        'moe_w_group': nrm((DEPTH, D_MODEL, N_GROUPS), D_MODEL ** -0.5),
        'moe_b_group': nrm((DEPTH, N_GROUPS), 0.01),
        'moe_w_expert': nrm((DEPTH, D_MODEL, N_EXPERTS), D_MODEL ** -0.5),
        'moe_b_expert': nrm((DEPTH, N_EXPERTS), 0.01),
        'moe_w_up': nrm((DEPTH, N_EXPERTS, D_MODEL, 2 * D_EXPERT), D_MODEL ** -0.5),
        'moe_w_down': nrm((DEPTH, N_EXPERTS, D_EXPERT, D_MODEL), D_EXPERT ** -0.5),
    }


def reference(x_prompt, x_sample, cache_mla_ckv, cache_mla_krope, state_retention, c, c_ctx,
              ada_w, ada_b, norm_mix, norm_ffn,
              ev_w_in, ev_q_a_norm, ev_w_q_up, ev_kv_a_norm, ev_w_kv_up, ev_q_norm, ev_k_norm,
              ev_ret_decay, ev_ret_gn, ev_w_out,
              od_w_in, od_dw, od_dw_b, od_ln_g, od_ln_b, od_w_out,
              moe_w_group, moe_b_group, moe_w_expert, moe_b_expert, moe_w_up, moe_w_down):
    xp, xs = x_prompt, x_sample
    new_ckv, new_krope, new_state = [], [], []
    for l in range(DEPTH):
        mp = modulation(c_ctx[None, :], ada_w[l], ada_b[l])
        ms = modulation(c, ada_w[l], ada_b[l])
        hp = modulate(rms_norm(xp, norm_mix[l]), mp[0], mp[1])
        hs = modulate(rms_norm(xs, norm_mix[l]), ms[0], ms[1])
        j = l // 2
        if l % 2 == 0:
            even = (ev_w_in[j], ev_q_a_norm[j], ev_w_q_up[j], ev_kv_a_norm[j], ev_w_kv_up[j],
                    ev_q_norm[j], ev_k_norm[j], ev_ret_decay[j], ev_ret_gn[j], ev_w_out[j])
            yp, ckv, kr, st = even_mixer_context(hp, *even)
            ys = even_mixer_latent(hs, cache_mla_ckv[:, j], cache_mla_krope[:, j], state_retention[:, j], *even)
            new_ckv.append(ckv)
            new_krope.append(kr)
            new_state.append(st)
        else:
            odd = (od_w_in[j], od_dw[j], od_dw_b[j], od_ln_g[j], od_ln_b[j], od_w_out[j])
            yp = conformer_conv(hp, *odd)
            ys = conformer_conv(hs, *odd)
        xp = xp + mp[2] * yp
        xs = xs + ms[2] * ys
        moe = (moe_w_group[l], moe_b_group[l], moe_w_expert[l], moe_b_expert[l], moe_w_up[l], moe_w_down[l])
        hp = modulate(rms_norm(xp, norm_ffn[l]), mp[3], mp[4])
        hs = modulate(rms_norm(xs, norm_ffn[l]), ms[3], ms[4])
        xp = xp + mp[5] * hier_moe(hp, *moe)
        xs = xs + ms[5] * hier_moe(hs, *moe)
    new_mla_ckv = jnp.stack(new_ckv, axis=1)
    new_mla_krope = jnp.stack(new_krope, axis=1)
    new_retention_state = jnp.stack(new_state, axis=1)
    return (xp, xs, new_mla_ckv, new_mla_krope, new_retention_state)
```

```python
import functools

import jax
import jax.numpy as jnp
from jax import lax
from jax.experimental import pallas as pl
from jax.experimental.pallas import tpu as pltpu

F32 = jnp.float32
BF16 = jnp.bfloat16

D_MODEL = 1024
BATCH, SEQ = 32, 256
DEC_BATCH, DEC_SEQ = 8, 4096
PAST_LEN = 256
GRID_W = 64
EPS = 1e-6
H_MLA, Q_RANK, KV_RANK = 8, 256, 128
MLA_NOPE, ROPE_DIM, MLA_V = 64, 32, 64
MLA_QK = MLA_NOPE + ROPE_DIM
ROPE_BASE = 10000.0
H_RET, RET_DK, RET_DV, RET_CHUNK = 4, 64, 128, 128
CONV_DIM, CONV_W = 1024, 31
N_GROUPS, EXP_PER_GROUP, D_EXPERT = 4, 8, 256
N_EXPERTS = N_GROUPS * EXP_PER_GROUP

LANES = 128
SUBLANES = 8
VMEM_LIMIT = 48 * 1024 * 1024

NP = BATCH * SEQ
NS = DEC_BATCH * DEC_SEQ
NT = NP + NS
TM = 256
NT_TILES = NT // TM
NP_TILES = NP // TM
S_TILES_PER_SEQ = DEC_SEQ // TM
LK_S = PAST_LEN + DEC_SEQ
KV_BLOCKS_PER_SEQ = LK_S // TM
N_PAIR = H_MLA // 2
EB = 128
CAP = NT + EB
CAP_BLOCKS = CAP // EB
N_ITEMS = (2 * NT) // EB + N_EXPERTS
HALO = 16
NEG = -1e30


def _cparams(sem):
    return pltpu.CompilerParams(dimension_semantics=sem, vmem_limit_bytes=VMEM_LIMIT)


def _full(shape):
    n = len(shape)
    return pl.BlockSpec(shape, lambda *_: (0,) * n)


def _rms(x, gain):
    return x * lax.rsqrt(jnp.mean(x * x, axis=-1, keepdims=True) + EPS) * gain


def _prenorm(x, gain, shift, scale):
    return _rms(x, gain) * (1.0 + scale) + shift


def _silu(x):
    return x * jax.nn.sigmoid(x)


def _rope128(x, cos, sin_lo, sin_hi):
    return x * cos + pltpu.roll(x, LANES - ROPE_DIM // 2, 1) * sin_lo + pltpu.roll(x, ROPE_DIM // 2, 1) * sin_hi


def _head_norm_rope(slab, gain, cos, sin_lo, sin_hi):
    r = lax.rsqrt(jnp.sum(slab * slab, axis=-1, keepdims=True) * (1.0 / MLA_QK) + EPS)
    return _rope128(slab * r * gain, cos, sin_lo, sin_hi)


def _ada_kernel(c_ref, w_ref, b_ref, o_ref):
    s = _silu(c_ref[...]).astype(BF16)
    o_ref[0] = jnp.dot(s, w_ref[0].astype(BF16), preferred_element_type=F32) + b_ref[0]


def _ada(cond, ada_w, ada_b):
    depth, d, n = ada_w.shape
    rows = cond.shape[0]
    tn = 1536
    return pl.pallas_call(
        _ada_kernel,
        grid=(depth, n // tn),
        in_specs=[pl.BlockSpec((rows, d), lambda l, j: (0, 0)),
                  pl.BlockSpec((1, d, tn), lambda l, j: (l, 0, j)),
                  pl.BlockSpec((1, 1, tn), lambda l, j: (l, 0, j))],
        out_specs=pl.BlockSpec((1, rows, tn), lambda l, j: (l, 0, j)),
        out_shape=jax.ShapeDtypeStruct((depth, rows, n), F32),
        compiler_params=_cparams(("arbitrary", "arbitrary")),
        name="ada_modulation",
    )(cond, ada_w, ada_b.reshape(depth, 1, n))


def _kv_heads(ckvn_bf, kr_slab, wk_ref, wv_ref, kg, cos, sin_lo, sin_hi, k_ref, v_ref):
    kk = jnp.dot(ckvn_bf, wk_ref[...], preferred_element_type=F32)
    vv = jnp.dot(ckvn_bf, wv_ref[...], preferred_element_type=F32)
    for h in range(H_MLA):
        kh = kk[:, h * LANES:(h + 1) * LANES] + kr_slab
        k_ref[h] = _head_norm_rope(kh, kg, cos, sin_lo, sin_hi).astype(BF16)
    for j in range(N_PAIR):
        v_ref[j] = vv[:, j * LANES:(j + 1) * LANES].astype(BF16)


def _even_in_kernel(x_ref, mod_ref, g_ref, win_ref, qan_ref, wq_ref, kvan_ref, wk_ref, wv_ref, qg_ref, kg_ref,
                    cos_ref, sl_ref, sh_ref,
                    q_ref, k_ref, v_ref, ckv_ref, kr_ref, rq_ref, rk_ref, rv_ref, rg_ref):
    h = _prenorm(x_ref[...], g_ref[...], mod_ref[0, 0:1, :], mod_ref[0, 1:2, :])
    z = jnp.dot(h.astype(BF16), win_ref[...], preferred_element_type=F32)
    cq = z[:, 0:256]
    ckv = z[:, 256:384]
    kr_slab = z[:, 384:512]
    cos, sin_lo, sin_hi = cos_ref[...], sl_ref[...], sh_ref[...]

    qq = jnp.dot(_rms(cq, qan_ref[...]).astype(BF16), wq_ref[...], preferred_element_type=F32)
    qg = qg_ref[...]
    for hh in range(H_MLA):
        q_ref[hh] = _head_norm_rope(qq[:, hh * LANES:(hh + 1) * LANES], qg, cos, sin_lo, sin_hi).astype(BF16)

    ckvn = _rms(ckv, kvan_ref[...])
    ckv_ref[...] = ckvn
    kr_ref[...] = kr_slab
    _kv_heads(ckvn.astype(BF16), kr_slab, wk_ref, wv_ref, kg_ref[...], cos, sin_lo, sin_hi, k_ref, v_ref)

    rq_ref[...] = z[:, 512:768].astype(BF16)
    rk_ref[...] = (z[:, 768:1024] * (RET_DK ** -0.5)).astype(BF16)
    rv_ref[...] = z[:, 1024:1536].astype(BF16)
    rg_ref[...] = z[:, 1536:2048]


def _even_in(x, mod, gain, wts, rope, *, n_tiles, x_blk, mod_row, rope_blk, kv_rows, kv_blk):
    win, qan, wq, kvan, wk, wv, qg, kg = wts
    cos, sin_lo, sin_hi = rope
    ntok = n_tiles * TM
    row = lambda f: (lambda i: (f(i), 0))
    tab = pl.BlockSpec((TM, LANES), row(rope_blk))
    heads = lambda f: (lambda i: (0, f(i), 0))
    out_shape = (
        jax.ShapeDtypeStruct((H_MLA, ntok, LANES), BF16),
        jax.ShapeDtypeStruct((H_MLA, kv_rows, LANES), BF16),
        jax.ShapeDtypeStruct((N_PAIR, kv_rows, LANES), BF16),
        jax.ShapeDtypeStruct((ntok, KV_RANK), F32),
        jax.ShapeDtypeStruct((ntok, LANES), F32),
        jax.ShapeDtypeStruct((ntok, H_RET * RET_DK), BF16),
        jax.ShapeDtypeStruct((ntok, H_RET * RET_DK), BF16),
        jax.ShapeDtypeStruct((ntok, H_RET * RET_DV), BF16),
        jax.ShapeDtypeStruct((ntok, H_RET * RET_DV), F32),
    )
    ident = lambda i: i
    out_specs = (
        pl.BlockSpec((H_MLA, TM, LANES), heads(ident)),
        pl.BlockSpec((H_MLA, TM, LANES), heads(kv_blk)),
        pl.BlockSpec((N_PAIR, TM, LANES), heads(kv_blk)),
        pl.BlockSpec((TM, KV_RANK), row(ident)),
        pl.BlockSpec((TM, LANES), row(ident)),
        pl.BlockSpec((TM, H_RET * RET_DK), row(ident)),
        pl.BlockSpec((TM, H_RET * RET_DK), row(ident)),
        pl.BlockSpec((TM, H_RET * RET_DV), row(ident)),
        pl.BlockSpec((TM, H_RET * RET_DV), row(ident)),
    )
    return pl.pallas_call(
        _even_in_kernel,
        grid=(n_tiles,),
        in_specs=[pl.BlockSpec((TM, D_MODEL), row(x_blk)),
                  pl.BlockSpec((1, 6, D_MODEL), lambda i: (mod_row(i), 0, 0)),
                  _full(gain.shape), _full(win.shape), _full(qan.shape), _full(wq.shape), _full(kvan.shape),
                  _full(wk.shape), _full(wv.shape), _full(qg.shape), _full(kg.shape), tab, tab, tab],
        out_specs=out_specs,
        out_shape=out_shape,
        compiler_params=_cparams(("arbitrary",)),
        name="even_in_proj",
    )(x, mod, gain, win, qan, wq, kvan, wk, wv, qg, kg, cos, sin_lo, sin_hi)


def _ctx_kv_kernel(ckv_ref, kr_ref, wk_ref, wv_ref, kg_ref, cos_ref, sl_ref, sh_ref, k_in, v_in, k_ref, v_ref):
    del k_in, v_in
    _kv_heads(ckv_ref[...].astype(BF16), kr_ref[...], wk_ref, wv_ref, kg_ref[...],
              cos_ref[...], sl_ref[...], sh_ref[...], k_ref, v_ref)


def _ctx_kv(cache_ckv, cache_kr_slab, wk, wv, kg, rope_ident, k_all, v_all):
    cos, sin_lo, sin_hi = rope_ident
    nb = cache_ckv.shape[0] // PAST_LEN
    blk = lambda b: (0, b * KV_BLOCKS_PER_SEQ, 0)
    any_spec = pl.BlockSpec(memory_space=pl.ANY)
    return pl.pallas_call(
        _ctx_kv_kernel,
        grid=(nb,),
        in_specs=[pl.BlockSpec((PAST_LEN, KV_RANK), lambda b: (b, 0)),
                  pl.BlockSpec((PAST_LEN, LANES), lambda b: (b, 0)),
                  _full(wk.shape), _full(wv.shape), _full(kg.shape),
                  _full(cos.shape), _full(cos.shape), _full(cos.shape), any_spec, any_spec],
        out_specs=(pl.BlockSpec((H_MLA, PAST_LEN, LANES), blk), pl.BlockSpec((N_PAIR, PAST_LEN, LANES), blk)),
        out_shape=(jax.ShapeDtypeStruct(k_all.shape, k_all.dtype), jax.ShapeDtypeStruct(v_all.shape, v_all.dtype)),
        input_output_aliases={8: 0, 9: 1},
        compiler_params=_cparams(("arbitrary",)),
        name="ctx_kv_heads",
    )(cache_ckv, cache_kr_slab, wk, wv, kg, cos, sin_lo, sin_hi, k_all, v_all)


def _attn_kernel(q_ref, k_ref, v_ref, *rest):
    o_ref = rest[-1]
    v = v_ref[0]
    outs = []
    for a in range(2):
        s = lax.dot_general(q_ref[a], k_ref[a], (((1,), (1,)), ((), ())), preferred_element_type=F32)
        p = jnp.exp(s - jnp.max(s, axis=-1, keepdims=True))
        l = jnp.sum(p, axis=-1, keepdims=True)
        outs.append(jnp.dot(p.astype(BF16), v, preferred_element_type=F32) / l)
    lane = lax.broadcasted_iota(jnp.int32, outs[0].shape, 1)
    o_ref[0] = jnp.where(lane < MLA_V, outs[0], outs[1]).astype(BF16)


def _attention(q, k, v, o_prev, *, nb, nq, lk, o_blk0):
    in_specs = [pl.BlockSpec((2, TM, LANES), lambda b, p, i: (p, b * nq + i, 0)),
                pl.BlockSpec((2, lk, LANES), lambda b, p, i: (p, b, 0)),
                pl.BlockSpec((1, lk, LANES), lambda b, p, i: (p, b, 0))]
    args = [q, k, v]
    aliases = {}
    if o_prev is not None:
        in_specs.append(pl.BlockSpec(memory_space=pl.ANY))
        args.append(o_prev)
        aliases = {3: 0}
    return pl.pallas_call(
        _attn_kernel,
        grid=(nb, N_PAIR, nq),
        in_specs=in_specs,
        out_specs=pl.BlockSpec((1, TM, LANES), lambda b, p, i: (p, o_blk0 + b * nq + i, 0)),
        out_shape=jax.ShapeDtypeStruct((N_PAIR, NT, LANES), BF16),
        input_output_aliases=aliases,
        compiler_params=_cparams(("arbitrary", "arbitrary", "arbitrary")),
        name="mla_attention",
    )(*args)


def _ret_kernel(dec_ref, q_ref, k_ref, v_ref, s0_ref, *rest, reverse, finish):
    if finish:
        of_ref, g_ref, gn_ref = rest[:3]
    o_ref, st_ref, s_scr = rest[-3:]
    j = pl.program_id(1)
    d = 1 if reverse else 0
    C = RET_CHUNK

    @pl.when(j == 0)
    def _():
        s_scr[...] = s0_ref[0]

    row = lax.broadcasted_iota(jnp.int32, (C, C), 0).astype(F32)
    col = lax.broadcasted_iota(jnp.int32, (C, C), 1).astype(F32)
    dist = (col - row) if reverse else (row - col)
    live = dist >= 0.0
    rowv = lax.broadcasted_iota(jnp.int32, (C, RET_DV), 0).astype(F32)
    rowk = lax.broadcasted_iota(jnp.int32, (C, RET_DK), 0).astype(F32)
    for h in range(H_RET):
        dl = jnp.full((1, LANES), dec_ref[d, h], F32)
        lg = jnp.minimum(dl, 0.0) - jnp.log1p(jnp.exp(-jnp.abs(dl)))
        q = q_ref[:, h * RET_DK:(h + 1) * RET_DK]
        k = k_ref[:, h * RET_DK:(h + 1) * RET_DK]
        v = v_ref[:, h * RET_DV:(h + 1) * RET_DV]
        dmask = jnp.where(live, jnp.exp(jnp.where(live, dist, 0.0) * lg), 0.0)
        scores = lax.dot_general(q, k, (((1,), (1,)), ((), ())), preferred_element_type=F32) * dmask
        inner = jnp.dot(scores.astype(BF16), v, preferred_element_type=F32)
        state = s_scr[h]
        xi = jnp.exp(((C - rowv) if reverse else (rowv + 1.0)) * lg)
        cross = jnp.dot(q, state.astype(BF16), preferred_element_type=F32) * xi
        wk = jnp.exp((rowk if reverse else (C - 1.0 - rowk)) * lg[:, :RET_DK])
        kw = (k.astype(F32) * wk).astype(BF16)
        u = lax.dot_general(kw, v, (((0,), (0,)), ((), ())), preferred_element_type=F32)
        new_state = jnp.exp(C * lg) * state + u
        s_scr[h] = new_state
        st_ref[0, h] = new_state
        o = inner + cross
        if finish:
            o = o + of_ref[:, h * RET_DV:(h + 1) * RET_DV]
            mu = jnp.mean(o, axis=-1, keepdims=True)
            var = jnp.mean(jnp.square(o - mu), axis=-1, keepdims=True)
            o = (o - mu) * lax.rsqrt(var + EPS) * gn_ref[:, h * RET_DV:(h + 1) * RET_DV]
            o = _silu(g_ref[:, h * RET_DV:(h + 1) * RET_DV]) * o
        o_ref[:, h * RET_DV:(h + 1) * RET_DV] = o.astype(o_ref.dtype)


def _retention_pass(decay, rq, rk, rv, s0, *, nb, nc, reverse, o_fwd=None, gate=None, gn=None, o_prev=None, o_blk0=0):
    C = RET_CHUNK
    chunk = (lambda b, j: (b * nc + (nc - 1 - j), 0)) if reverse else (lambda b, j: (b * nc + j, 0))
    in_specs = [pl.BlockSpec(memory_space=pltpu.SMEM),
                pl.BlockSpec((C, H_RET * RET_DK), chunk),
                pl.BlockSpec((C, H_RET * RET_DK), chunk),
                pl.BlockSpec((C, H_RET * RET_DV), chunk),
                pl.BlockSpec((1, H_RET, RET_DK, RET_DV), lambda b, j: (b, 0, 0, 0))]
    args = [decay, rq, rk, rv, s0]
    finish = o_fwd is not None
    aliases = {}
    if finish:
        in_specs += [pl.BlockSpec((C, H_RET * RET_DV), chunk), pl.BlockSpec((C, H_RET * RET_DV), chunk),
                     _full(gn.shape)]
        args += [o_fwd, gate, gn]
        if o_prev is not None:
            in_specs.append(pl.BlockSpec(memory_space=pl.ANY))
            args.append(o_prev)
            aliases = {len(args) - 1: 0}
        o_spec = pl.BlockSpec((C, H_RET * RET_DV), lambda b, j: (o_blk0 + b * nc + (nc - 1 - j), 0))
        o_shape = jax.ShapeDtypeStruct((NT, H_RET * RET_DV), BF16)
    else:
        o_spec = pl.BlockSpec((C, H_RET * RET_DV), chunk)
        o_shape = jax.ShapeDtypeStruct((nb * nc * C, H_RET * RET_DV), F32)
    return pl.pallas_call(
        functools.partial(_ret_kernel, reverse=reverse, finish=finish),
        grid=(nb, nc),
        in_specs=in_specs,
        out_specs=(o_spec, pl.BlockSpec((1, H_RET, RET_DK, RET_DV), lambda b, j: (b, 0, 0, 0))),
        out_shape=(o_shape, jax.ShapeDtypeStruct((nb, H_RET, RET_DK, RET_DV), F32)),
        scratch_shapes=[pltpu.VMEM((H_RET, RET_DK, RET_DV), F32)],
        input_output_aliases=aliases,
        compiler_params=_cparams(("arbitrary", "arbitrary")),
        name="retention_bwd" if reverse else "retention_fwd",
    )(*args)


def _residual_and_route(x, y, mod_ref, gffn_ref, wrh_ref, wrl_ref, br_ref, cnt_scr, x1_ref, h2_ref, route_ref, cnt_ref):
    i = pl.program_id(0)

    @pl.when(i == 0)
    def _():
        cnt_scr[...] = jnp.zeros_like(cnt_scr)

    x1 = x + mod_ref[0, 2:3, :] * y
    x1_ref[...] = x1
    h2 = _prenorm(x1, gffn_ref[...], mod_ref[0, 3:4, :], mod_ref[0, 4:5, :])
    h2_ref[...] = h2
    hi = h2.astype(BF16)
    lo = (h2 - hi.astype(F32)).astype(BF16)
    wrh = wrh_ref[...]
    lg = (jnp.dot(hi, wrh, preferred_element_type=F32) + jnp.dot(lo, wrh, preferred_element_type=F32)
          + jnp.dot(hi, wrl_ref[...], preferred_element_type=F32) + br_ref[...])
    lane = lax.broadcasted_iota(jnp.int32, lg.shape, 1).astype(F32)
    big = float(4 * LANES)
    gl = jnp.where((lane >= N_EXPERTS) & (lane < N_EXPERTS + N_GROUPS), lg, NEG)
    gmax = jnp.max(gl, axis=-1, keepdims=True)
    g_w = 1.0 / jnp.sum(jnp.exp(gl - gmax), axis=-1, keepdims=True)
    g_lane = jnp.min(jnp.where(gl == gmax, lane, big), axis=-1, keepdims=True)
    e_lo = (g_lane - N_EXPERTS) * EXP_PER_GROUP
    el = jnp.where((lane >= e_lo) & (lane < e_lo + EXP_PER_GROUP), lg, NEG)
    m1 = jnp.max(el, axis=-1, keepdims=True)
    i1 = jnp.min(jnp.where(el == m1, lane, big), axis=-1, keepdims=True)
    el2 = jnp.where(lane == i1, NEG, el)
    m2 = jnp.max(el2, axis=-1, keepdims=True)
    i2 = jnp.min(jnp.where(el2 == m2, lane, big), axis=-1, keepdims=True)
    t = jnp.exp(m2 - m1)
    w1 = g_w / (1.0 + t)
    w2 = w1 * t
    oh1 = lane == i1
    oh2 = lane == i2
    oh = jnp.where(oh1 | oh2, 1.0, 0.0)
    rr = lax.broadcasted_iota(jnp.int32, (TM, TM), 0)
    cc = lax.broadcasted_iota(jnp.int32, (TM, TM), 1)
    strict_lower = jnp.where(cc < rr, 1.0, 0.0).astype(BF16)
    before = jnp.dot(strict_lower, oh.astype(BF16), preferred_element_type=F32) + cnt_scr[...]
    slot1 = jnp.sum(jnp.where(oh1, before, 0.0), axis=-1, keepdims=True) + i1 * CAP
    slot2 = jnp.sum(jnp.where(oh2, before, 0.0), axis=-1, keepdims=True) + i2 * CAP
    cnt = cnt_scr[...] + jnp.sum(oh, axis=0, keepdims=True)
    cnt_scr[...] = cnt
    cnt_ref[...] = cnt
    route = jnp.where(lane == 0.0, i1, jnp.where(lane == 1.0, i2, jnp.where(lane == 2.0, w1, jnp.where(
        lane == 3.0, w2, jnp.where(lane == 4.0, slot1, jnp.where(lane == 5.0, slot2, 0.0))))))
    route_ref[...] = route


def _tail_specs(x_blk):
    row = lambda i: (i, 0)
    in_specs = [pl.BlockSpec((1, 6, D_MODEL), lambda i: (_cond_row(i), 0, 0)),
                _full((1, D_MODEL)), _full((D_MODEL, LANES)), _full((D_MODEL, LANES)), _full((1, LANES))]
    out_specs = (pl.BlockSpec((TM, D_MODEL), row), pl.BlockSpec((TM, D_MODEL), row),
                 pl.BlockSpec((TM, LANES), row), _full((1, LANES)))
    out_shape = (jax.ShapeDtypeStruct((NT, D_MODEL), F32), jax.ShapeDtypeStruct((NT, D_MODEL), F32),
                 jax.ShapeDtypeStruct((NT, LANES), F32), jax.ShapeDtypeStruct((1, LANES), F32))
    return in_specs, out_specs, out_shape


def _cond_row(i):
    return jnp.where(i < NP_TILES, DEC_BATCH, (i - NP_TILES) // S_TILES_PER_SEQ)


def _even_out_kernel(x_ref, o_ref, r_ref, wout_ref, mod_ref, gffn_ref, wrh_ref, wrl_ref, br_ref,
                     x1_ref, h2_ref, route_ref, cnt_ref, cnt_scr):
    y = jnp.dot(r_ref[...], wout_ref[H_MLA * MLA_V:, :], preferred_element_type=F32)
    for p in range(N_PAIR):
        y = y + jnp.dot(o_ref[p], wout_ref[p * LANES:(p + 1) * LANES, :], preferred_element_type=F32)
    _residual_and_route(x_ref[...], y, mod_ref, gffn_ref, wrh_ref, wrl_ref, br_ref, cnt_scr,
                        x1_ref, h2_ref, route_ref, cnt_ref)


def _even_out(x, o_mla, o_ret, wout, mod, gffn, wrh, wrl, br):
    row = lambda i: (i, 0)
    tail_in, out_specs, out_shape = _tail_specs(None)
    return pl.pallas_call(
        _even_out_kernel,
        grid=(NT_TILES,),
        in_specs=[pl.BlockSpec((TM, D_MODEL), row),
                  pl.BlockSpec((N_PAIR, TM, LANES), lambda i: (0, i, 0)),
                  pl.BlockSpec((TM, H_RET * RET_DV), row),
                  _full(wout.shape)] + tail_in,
        out_specs=out_specs,
        out_shape=out_shape,
        scratch_shapes=[pltpu.VMEM((1, LANES), F32)],
        compiler_params=_cparams(("arbitrary",)),
        name="even_out_route",
    )(x, o_mla, o_ret, wout, mod, gffn, wrh, wrl, br)


def _conf_in_kernel(x_ref, mod_ref, g_ref, win_ref, u_ref):
    h = _prenorm(x_ref[...], g_ref[...], mod_ref[0, 0:1, :], mod_ref[0, 1:2, :])
    z = jnp.dot(h.astype(BF16), win_ref[...], preferred_element_type=F32)
    u_ref[...] = z[:, :CONV_DIM] * jax.nn.sigmoid(z[:, CONV_DIM:])


def _conf_in(x, mod, gain, win):
    row = lambda i: (i, 0)
    return pl.pallas_call(
        _conf_in_kernel,
        grid=(NT_TILES,),
        in_specs=[pl.BlockSpec((TM, D_MODEL), row),
                  pl.BlockSpec((1, 6, D_MODEL), lambda i: (_cond_row(i), 0, 0)),
                  _full(gain.shape), _full(win.shape)],
        out_specs=pl.BlockSpec((TM, CONV_DIM), row),
        out_shape=jax.ShapeDtypeStruct((NT, CONV_DIM), F32),
        compiler_params=_cparams(("arbitrary",)),
        name="conformer_in_glu",
    )(x, mod, gain, win)


CONV_ROWS = 32


def _conv_out_kernel(x_ref, u_ref, ul_ref, ur_ref, dw_ref, dwb_ref, lng_ref, lnb_ref, wout_ref,
                     mod_ref, gffn_ref, wrh_ref, wrl_ref, br_ref,
                     x1_ref, h2_ref, route_ref, cnt_ref, cnt_scr, ext_scr, act_scr):
    i = pl.program_id(0)
    t = (i - NP_TILES) % S_TILES_PER_SEQ
    first = (i < NP_TILES) | (t == 0)
    last = (i < NP_TILES) | (t == S_TILES_PER_SEQ - 1)
    ext_scr[0:HALO, :] = jnp.where(first, 0.0, ul_ref[...])
    ext_scr[HALO:HALO + TM, :] = u_ref[...]
    ext_scr[HALO + TM:, :] = jnp.where(last, 0.0, ur_ref[...])
    off = HALO - CONV_W // 2
    for rb in range(TM // CONV_ROWS):
        acc = jnp.zeros((CONV_ROWS, CONV_DIM), F32)
        for kk in range(CONV_W):
            acc = acc + ext_scr[pl.ds(rb * CONV_ROWS + off + kk, CONV_ROWS), :] * dw_ref[kk:kk + 1, :]
        c = acc + dwb_ref[...]
        mu = jnp.mean(c, axis=-1, keepdims=True)
        var = jnp.mean(jnp.square(c - mu), axis=-1, keepdims=True)
        c = (c - mu) * lax.rsqrt(var + EPS) * lng_ref[...] + lnb_ref[...]
        act_scr[rb * CONV_ROWS:(rb + 1) * CONV_ROWS, :] = _silu(c).astype(BF16)
    y = jnp.dot(act_scr[...], wout_ref[...], preferred_element_type=F32)
    _residual_and_route(x_ref[...], y, mod_ref, gffn_ref, wrh_ref, wrl_ref, br_ref, cnt_scr,
                        x1_ref, h2_ref, route_ref, cnt_ref)


def _conv_out(x, u, dw, dwb, lng, lnb, wout, mod, gffn, wrh, wrl, br):
    row = lambda i: (i, 0)
    per = TM // HALO
    n_halo = NT // HALO
    tail_in, out_specs, out_shape = _tail_specs(None)
    return pl.pallas_call(
        _conv_out_kernel,
        grid=(NT_TILES,),
        in_specs=[pl.BlockSpec((TM, D_MODEL), row),
                  pl.BlockSpec((TM, CONV_DIM), row),
                  pl.BlockSpec((HALO, CONV_DIM), lambda i: (jnp.maximum(i * per - 1, 0), 0)),
                  pl.BlockSpec((HALO, CONV_DIM), lambda i: (jnp.minimum((i + 1) * per, n_halo - 1), 0)),
                  _full(dw.shape), _full(dwb.shape), _full(lng.shape), _full(lnb.shape), _full(wout.shape)] + tail_in,
        out_specs=out_specs,
        out_shape=out_shape,
        scratch_shapes=[pltpu.VMEM((1, LANES), F32), pltpu.VMEM((TM + 2 * HALO, CONV_DIM), F32),
                        pltpu.VMEM((TM, CONV_DIM), BF16)],
        compiler_params=_cparams(("arbitrary",)),
        name="conformer_conv_out_route",
    )(x, u, u, u, dw, dwb, lng, lnb, wout, mod, gffn, wrh, wrl, br)


def _row_copy(src_ref, s, dst_ref, d, sem):
    return pltpu.make_async_copy(src_ref.at[pl.ds(s, 1), :], dst_ref.at[pl.ds(d, 1), :], sem)


def _dispatch_kernel(slot_ref, h_ref, xbuf_ref, sem):
    base = pl.program_id(0) * (2 * TM)

    def start(r, c):
        _row_copy(h_ref, r, xbuf_ref, slot_ref[base + 2 * r], sem).start()
        _row_copy(h_ref, r, xbuf_ref, slot_ref[base + 2 * r + 1], sem).start()
        return c

    def wait(r, c):
        _row_copy(h_ref, r, xbuf_ref, slot_ref[base + 2 * r], sem).wait()
        _row_copy(h_ref, r, xbuf_ref, slot_ref[base + 2 * r + 1], sem).wait()
        return c

    lax.fori_loop(0, TM, start, 0)
    lax.fori_loop(0, TM, wait, 0)


def _dispatch(slots, h2):
    return pl.pallas_call(
        _dispatch_kernel,
        grid_spec=pltpu.PrefetchScalarGridSpec(
            num_scalar_prefetch=1,
            grid=(NT_TILES,),
            in_specs=[pl.BlockSpec((TM, D_MODEL), lambda i, s: (i, 0))],
            out_specs=pl.BlockSpec(memory_space=pl.ANY),
            scratch_shapes=[pltpu.SemaphoreType.DMA(())]),
        out_shape=jax.ShapeDtypeStruct((N_EXPERTS * CAP, D_MODEL), F32),
        compiler_params=_cparams(("arbitrary",)),
        name="moe_dispatch",
    )(slots, h2)


def _pad_fill_kernel(cnt_ref, xin_ref, xbuf_ref, zero_scr, sem):
    del xin_ref
    e = pl.program_id(0)
    n = cnt_ref[e]
    pad = (EB - n % EB) % EB
    zero_scr[...] = jnp.zeros_like(zero_scr)

    def start(r, c):
        @pl.when(r < pad)
        def _():
            _row_copy(zero_scr, 0, xbuf_ref, e * CAP + n + r, sem).start()
        return c

    def wait(r, c):
        @pl.when(r < pad)
        def _():
            _row_copy(zero_scr, 0, xbuf_ref, e * CAP + n + r, sem).wait()
        return c

    lax.fori_loop(0, EB - 1, start, 0)
    lax.fori_loop(0, EB - 1, wait, 0)


def _pad_fill(counts, xbuf):
    return pl.pallas_call(
        _pad_fill_kernel,
        grid_spec=pltpu.PrefetchScalarGridSpec(
            num_scalar_prefetch=1,
            grid=(N_EXPERTS,),
            in_specs=[pl.BlockSpec(memory_space=pl.ANY)],
            out_specs=pl.BlockSpec(memory_space=pl.ANY),
            scratch_shapes=[pltpu.VMEM((SUBLANES, D_MODEL), F32), pltpu.SemaphoreType.DMA(())]),
        out_shape=jax.ShapeDtypeStruct(xbuf.shape, xbuf.dtype),
        input_output_aliases={1: 0},
        compiler_params=_cparams(("arbitrary",)),
        name="moe_pad_fill",
    )(counts, xbuf)


def _expert_kernel(blk_ref, exp_ref, flag_ref, x_ref, wu_ref, wd_ref, o_ref, wu_scr, wd_scr):
    i = pl.program_id(0)
    flag = flag_ref[i]

    @pl.when((flag & 2) != 0)
    def _():
        wu_scr[...] = wu_ref[0].astype(BF16)
        wd_scr[...] = wd_ref[0].astype(BF16)

    @pl.when((flag & 1) != 0)
    def _():
        ab = jnp.dot(x_ref[...].astype(BF16), wu_scr[...], preferred_element_type=F32)
        mid = (_silu(ab[:, :D_EXPERT]) * ab[:, D_EXPERT:]).astype(BF16)
        o_ref[...] = jnp.dot(mid, wd_scr[...], preferred_element_type=F32)


def _experts(blk, exp, flag, xbuf, w_up, w_down):
    return pl.pallas_call(
        _expert_kernel,
        grid_spec=pltpu.PrefetchScalarGridSpec(
            num_scalar_prefetch=3,
            grid=(N_ITEMS,),
            in_specs=[pl.BlockSpec((EB, D_MODEL), lambda i, b, e, f: (b[i], 0)),
                      pl.BlockSpec((1, D_MODEL, 2 * D_EXPERT), lambda i, b, e, f: (e[i], 0, 0)),
                      pl.BlockSpec((1, D_EXPERT, D_MODEL), lambda i, b, e, f: (e[i], 0, 0))],
            out_specs=pl.BlockSpec((EB, D_MODEL), lambda i, b, e, f: (b[i], 0)),
            scratch_shapes=[pltpu.VMEM((D_MODEL, 2 * D_EXPERT), BF16), pltpu.VMEM((D_EXPERT, D_MODEL), BF16)]),
        out_shape=jax.ShapeDtypeStruct((N_EXPERTS * CAP, D_MODEL), F32),
        compiler_params=_cparams(("arbitrary",)),
        name="moe_experts",
    )(blk, exp, flag, xbuf, w_up, w_down)


def _combine_kernel(slot_ref, x_ref, route_ref, mod_ref, ybuf_ref, o_ref, g_scr, sem):
    base = pl.program_id(0) * (2 * TM)

    def start(r, c):
        _row_copy(ybuf_ref, slot_ref[base + 2 * r], g_scr, r, sem).start()
        _row_copy(ybuf_ref, slot_ref[base + 2 * r + 1], g_scr, TM + r, sem).start()
        return c

    def wait(r, c):
        _row_copy(ybuf_ref, slot_ref[base + 2 * r], g_scr, r, sem).wait()
        _row_copy(ybuf_ref, slot_ref[base + 2 * r + 1], g_scr, TM + r, sem).wait()
        return c

    lax.fori_loop(0, TM, start, 0)
    lax.fori_loop(0, TM, wait, 0)
    route = route_ref[...]
    y = route[:, 2:3] * g_scr[0:TM, :] + route[:, 3:4] * g_scr[TM:2 * TM, :]
    o_ref[...] = x_ref[...] + mod_ref[0, 5:6, :] * y


def _combine(slots, x1, route, mod, ybuf):
    row = lambda i, s: (i, 0)
    return pl.pallas_call(
        _combine_kernel,
        grid_spec=pltpu.PrefetchScalarGridSpec(
            num_scalar_prefetch=1,
            grid=(NT_TILES,),
            in_specs=[pl.BlockSpec((TM, D_MODEL), row),
                      pl.BlockSpec((TM, LANES), row),
                      pl.BlockSpec((1, 6, D_MODEL), lambda i, s: (_cond_row(i), 0, 0)),
                      pl.BlockSpec(memory_space=pl.ANY)],
            out_specs=pl.BlockSpec((TM, D_MODEL), row),
            scratch_shapes=[pltpu.VMEM((2 * TM, D_MODEL), F32), pltpu.SemaphoreType.DMA(())]),
        out_shape=jax.ShapeDtypeStruct((NT, D_MODEL), F32),
        compiler_params=_cparams(("arbitrary",)),
        name="moe_combine",
    )(slots, x1, route, mod, ybuf)


def _moe(x1, h2, route, counts, mod, w_up, w_down):
    slots = route[:, 4:6].astype(jnp.int32).reshape(-1)
    cnt = counts[0, :N_EXPERTS].astype(jnp.int32)
    nblk = (cnt + EB - 1) // EB
    ends = jnp.cumsum(nblk)
    total = ends[-1]
    item = jnp.arange(N_ITEMS, dtype=jnp.int32)
    valid = item < total
    item_c = jnp.minimum(item, total - 1)
    exp = jnp.minimum(jnp.searchsorted(ends, item_c, side='right'), N_EXPERTS - 1).astype(jnp.int32)
    j = item_c - (ends[exp] - nblk[exp])
    blk = (exp * CAP_BLOCKS + j).astype(jnp.int32)
    flag = (valid.astype(jnp.int32) + 2 * (valid & (j == 0)).astype(jnp.int32)).astype(jnp.int32)

    xbuf = _dispatch(slots, h2)
    xbuf = _pad_fill(cnt, xbuf)
    ybuf = _experts(blk, exp, flag, xbuf, w_up, w_down)
    return _combine(slots, x1, route, mod, ybuf)


def _pad_heads(w, width, real):
    lead = w.shape[:-1]
    w = w.reshape(lead + (H_MLA, real))
    w = jnp.pad(w, [(0, 0)] * len(lead) + [(0, 0), (0, width - real)])
    return w.reshape(lead + (H_MLA * width,))


def _rope_tables():
    L = DEC_SEQ
    rows = L // GRID_W
    r = jnp.repeat(jnp.arange(rows, dtype=F32), GRID_W)
    col = jnp.tile(jnp.arange(GRID_W, dtype=F32), rows)
    n_f = ROPE_DIM // 4
    freqs = ROPE_BASE ** (-jnp.arange(n_f, dtype=F32) / n_f)
    ang = jnp.concatenate([r[:, None] * freqs, col[:, None] * freqs], axis=-1)
    cos, sin = jnp.cos(ang), jnp.sin(ang)
    half = ROPE_DIM // 2
    z = lambda n: jnp.zeros((L, n), F32)
    o = lambda n: jnp.ones((L, n), F32)
    cos_t = jnp.concatenate([o(MLA_NOPE), cos, cos, o(LANES - MLA_QK)], axis=-1)
    sin_lo = jnp.concatenate([z(MLA_NOPE), -sin, z(half), z(LANES - MLA_QK)], axis=-1)
    sin_hi = jnp.concatenate([z(MLA_NOPE), z(half), sin, z(LANES - MLA_QK)], axis=-1)
    ident = (jnp.ones((TM, LANES), F32), jnp.zeros((TM, LANES), F32), jnp.zeros((TM, LANES), F32))
    return (cos_t, sin_lo, sin_hi), ident


def _even_weights(w_in, q_a_norm, w_q_up, kv_a_norm, w_kv_up, q_norm, k_norm):
    d = D_MODEL
    kr_cols = jnp.zeros((d, LANES), F32).at[:, MLA_NOPE:MLA_QK].set(w_in[:, Q_RANK + KV_RANK:Q_RANK + KV_RANK + ROPE_DIM])
    rest = w_in[:, Q_RANK + KV_RANK + ROPE_DIM:]
    win = jnp.concatenate([w_in[:, :Q_RANK + KV_RANK], kr_cols, rest], axis=-1).astype(BF16)
    wq = _pad_heads(w_q_up, LANES, MLA_QK).astype(BF16)
    kv = w_kv_up.reshape(KV_RANK, H_MLA, MLA_NOPE + MLA_V)
    wk = _pad_heads(kv[:, :, :MLA_NOPE].reshape(KV_RANK, H_MLA * MLA_NOPE), LANES, MLA_NOPE).astype(BF16)
    wv = kv[:, :, MLA_NOPE:].reshape(KV_RANK, H_MLA * MLA_V).astype(BF16)
    padg = lambda g: jnp.pad(g, (0, LANES - MLA_QK)).reshape(1, LANES)
    qg = padg(q_norm) * (MLA_QK ** -0.5)
    kg = padg(k_norm)
    return (win, q_a_norm.reshape(1, -1), wq, kv_a_norm.reshape(1, -1), wk, wv, qg, kg)


def _router_weights(w_group, b_group, w_expert, b_expert):
    w = jnp.zeros((D_MODEL, LANES), F32).at[:, :N_EXPERTS].set(w_expert).at[:, N_EXPERTS:N_EXPERTS + N_GROUPS].set(w_group)
    b = jnp.zeros((1, LANES), F32).at[0, :N_EXPERTS].set(b_expert).at[0, N_EXPERTS:N_EXPERTS + N_GROUPS].set(b_group)
    hi = w.astype(BF16)
    lo = (w - hi.astype(F32)).astype(BF16)
    return hi, lo, b


def kernel(x_prompt, x_sample, cache_mla_ckv, cache_mla_krope, state_retention, c, c_ctx, ada_w, ada_b, norm_mix, norm_ffn, ev_w_in, ev_q_a_norm, ev_w_q_up, ev_kv_a_norm, ev_w_kv_up, ev_q_norm, ev_k_norm, ev_ret_decay, ev_ret_gn, ev_w_out, od_w_in, od_dw, od_dw_b, od_ln_g, od_ln_b, od_w_out, moe_w_group, moe_b_group, moe_w_expert, moe_b_expert, moe_w_up, moe_w_down):
    depth = ada_w.shape[0]
    x = jnp.concatenate([x_prompt.reshape(NP, D_MODEL), x_sample.reshape(NS, D_MODEL)], axis=0)
    cond = jnp.concatenate([c, c_ctx[None, :], jnp.zeros((2 * SUBLANES - DEC_BATCH - 1, D_MODEL), F32)], axis=0)
    mods = _ada(cond, ada_w, ada_b).reshape(depth, cond.shape[0], 6, D_MODEL)
    rope, rope_ident = _rope_tables()
    new_ckv, new_krope, new_state = [], [], []

    for l in range(depth):
        mod = mods[l]
        jj = l // 2
        router = _router_weights(moe_w_group[l], moe_b_group[l], moe_w_expert[l], moe_b_expert[l])
        gmix = norm_mix[l].reshape(1, D_MODEL)
        gffn = norm_ffn[l].reshape(1, D_MODEL)
        if l % 2 == 0:
            wts = _even_weights(ev_w_in[jj], ev_q_a_norm[jj], ev_w_q_up[jj], ev_kv_a_norm[jj], ev_w_kv_up[jj],
                                ev_q_norm[jj], ev_k_norm[jj])
            qp, kp, vp, ckv_p, kr_p, rq_p, rk_p, rv_p, rg_p = _even_in(
                x, mod, gmix, wts, rope_ident, n_tiles=NP_TILES, x_blk=lambda i: i, mod_row=lambda i: DEC_BATCH,
                rope_blk=lambda i: 0, kv_rows=NP, kv_blk=lambda i: i)
            qs, ks, vs, _, _, rq_s, rk_s, rv_s, rg_s = _even_in(
                x, mod, gmix, wts, rope, n_tiles=NS // TM, x_blk=lambda i: NP_TILES + i,
                mod_row=lambda i: i // S_TILES_PER_SEQ, rope_blk=lambda i: i % S_TILES_PER_SEQ,
                kv_rows=DEC_BATCH * LK_S,
                kv_blk=lambda i: (i // S_TILES_PER_SEQ) * KV_BLOCKS_PER_SEQ + 1 + i % S_TILES_PER_SEQ)
            kr_cache = jnp.pad(cache_mla_krope[:, jj].reshape(DEC_BATCH * PAST_LEN, ROPE_DIM),
                               ((0, 0), (MLA_NOPE, LANES - MLA_QK)))
            ks, vs = _ctx_kv(cache_mla_ckv[:, jj].reshape(DEC_BATCH * PAST_LEN, KV_RANK), kr_cache,
                             wts[4], wts[5], wts[7], rope_ident, ks, vs)
            o_mla = _attention(qp, kp, vp, None, nb=BATCH, nq=1, lk=SEQ, o_blk0=0)
            o_mla = _attention(qs, ks, vs, o_mla, nb=DEC_BATCH, nq=S_TILES_PER_SEQ, lk=LK_S, o_blk0=NP_TILES)

            decay = ev_ret_decay[jj]
            gn = ev_ret_gn[jj].reshape(1, -1)
            zero_state = jnp.zeros((BATCH, H_RET, RET_DK, RET_DV), F32)
            ncp, ncs = SEQ // RET_CHUNK, DEC_SEQ // RET_CHUNK
            of_p, sf_p = _retention_pass(decay, rq_p, rk_p, rv_p, zero_state, nb=BATCH, nc=ncp, reverse=False)
            o_ret, sb_p = _retention_pass(decay, rq_p, rk_p, rv_p, zero_state, nb=BATCH, nc=ncp, reverse=True,
                                          o_fwd=of_p, gate=rg_p, gn=gn)
            of_s, _ = _retention_pass(decay, rq_s, rk_s, rv_s, state_retention[:, jj, 0], nb=DEC_BATCH, nc=ncs,
                                      reverse=False)
            o_ret, _ = _retention_pass(decay, rq_s, rk_s, rv_s, state_retention[:, jj, 1], nb=DEC_BATCH, nc=ncs,
                                       reverse=True, o_fwd=of_s, gate=rg_s, gn=gn, o_prev=o_ret,
                                       o_blk0=NP // RET_CHUNK)
            new_ckv.append(ckv_p.reshape(BATCH, SEQ, KV_RANK))
            new_krope.append(kr_p[:, MLA_NOPE:MLA_QK].reshape(BATCH, SEQ, ROPE_DIM))
            new_state.append(jnp.stack([sf_p, sb_p], axis=1))
            x1, h2, route, counts = _even_out(x, o_mla, o_ret, ev_w_out[jj].astype(BF16), mod, gffn, *router)
        else:
            u = _conf_in(x, mod, gmix, od_w_in[jj].astype(BF16))
            x1, h2, route, counts = _conv_out(
                x, u, od_dw[jj], od_dw_b[jj].reshape(1, -1), od_ln_g[jj].reshape(1, -1), od_ln_b[jj].reshape(1, -1),
                od_w_out[jj].astype(BF16), mod, gffn, *router)
        x = _moe(x1, h2, route, counts, mod, moe_w_up[l], moe_w_down[l])

    y_prompt = x[:NP].reshape(BATCH, SEQ, D_MODEL)
    y_sample = x[NP:].reshape(DEC_BATCH, DEC_SEQ, D_MODEL)
    return (y_prompt, y_sample, jnp.stack(new_ckv, axis=1), jnp.stack(new_krope, axis=1),
            jnp.stack(new_state, axis=1))
```

```python
import functools

import jax
import jax.numpy as jnp
from jax import lax
from jax.experimental import pallas as pl
from jax.experimental.pallas import tpu as pltpu

F32 = jnp.float32
BF16 = jnp.bfloat16
U32 = jnp.uint32

D_MODEL = 1024
BATCH, SEQ = 32, 256
DEC_BATCH, DEC_SEQ = 8, 4096
PAST_LEN = 256
GRID_W = 64
EPS = 1e-6
H_MLA, Q_RANK, KV_RANK = 8, 256, 128
MLA_NOPE, ROPE_DIM, MLA_V = 64, 32, 64
MLA_QK = MLA_NOPE + ROPE_DIM
ROPE_BASE = 10000.0
H_RET, RET_DK, RET_DV, RET_CHUNK = 4, 64, 128, 128
CONV_DIM, CONV_W = 1024, 31
N_GROUPS, EXP_PER_GROUP, D_EXPERT = 4, 8, 256
N_EXPERTS = N_GROUPS * EXP_PER_GROUP

LANES = 128
SUBLANES = 8
VMEM_LIMIT = 48 * 1024 * 1024

NP = BATCH * SEQ
NS = DEC_BATCH * DEC_SEQ
NT = NP + NS
TM = 256
NT_TILES = NT // TM
NP_TILES = NP // TM
S_TILES_PER_SEQ = DEC_SEQ // TM
LK_S = PAST_LEN + DEC_SEQ
KV_BLOCKS_PER_SEQ = LK_S // TM
N_PAIR = H_MLA // 2
EB = 256
CAP = NT + EB
CAP_BLOCKS = CAP // EB
N_ITEMS = (2 * NT) // EB + N_EXPERTS
HALO = 16
PACKED = D_MODEL // 2
ISSUE_UNROLL = 8
NEG = -1e30


def _cparams(sem):
    return pltpu.CompilerParams(dimension_semantics=sem, vmem_limit_bytes=VMEM_LIMIT)


def _full(shape):
    n = len(shape)
    return pl.BlockSpec(shape, lambda *_: (0,) * n)


def _rms(x, gain):
    return x * lax.rsqrt(jnp.mean(x * x, axis=-1, keepdims=True) + EPS) * gain


def _prenorm(x, gain, shift, scale):
    return _rms(x, gain) * (1.0 + scale) + shift


def _silu(x):
    return x * jax.nn.sigmoid(x)


def _rope128(x, cos, sin_lo, sin_hi):
    return x * cos + pltpu.roll(x, LANES - ROPE_DIM // 2, 1) * sin_lo + pltpu.roll(x, ROPE_DIM // 2, 1) * sin_hi


def _head_norm_rope(slab, gain, cos, sin_lo, sin_hi):
    r = lax.rsqrt(jnp.sum(slab * slab, axis=-1, keepdims=True) * (1.0 / MLA_QK) + EPS)
    return _rope128(slab * r * gain, cos, sin_lo, sin_hi)


def _ada_kernel(c_ref, w_ref, b_ref, o_ref):
    s = _silu(c_ref[...]).astype(BF16)
    o_ref[0] = jnp.dot(s, w_ref[0].astype(BF16), preferred_element_type=F32) + b_ref[0]


def _ada(cond, ada_w, ada_b):
    depth, d, n = ada_w.shape
    rows = cond.shape[0]
    tn = 1536
    return pl.pallas_call(
        _ada_kernel,
        grid=(depth, n // tn),
        in_specs=[pl.BlockSpec((rows, d), lambda l, j: (0, 0)),
                  pl.BlockSpec((1, d, tn), lambda l, j: (l, 0, j)),
                  pl.BlockSpec((1, 1, tn), lambda l, j: (l, 0, j))],
        out_specs=pl.BlockSpec((1, rows, tn), lambda l, j: (l, 0, j)),
        out_shape=jax.ShapeDtypeStruct((depth, rows, n), F32),
        compiler_params=_cparams(("arbitrary", "arbitrary")),
        name="ada_modulation",
    )(cond, ada_w, ada_b.reshape(depth, 1, n))


def _kv_heads(ckvn_bf, kr_slab, wk_ref, wv_ref, kg, cos, sin_lo, sin_hi, k_ref, v_ref):
    kk = jnp.dot(ckvn_bf, wk_ref[...], preferred_element_type=F32)
    vv = jnp.dot(ckvn_bf, wv_ref[...], preferred_element_type=F32)
    for h in range(H_MLA):
        kh = kk[:, h * LANES:(h + 1) * LANES] + kr_slab
        k_ref[h] = _head_norm_rope(kh, kg, cos, sin_lo, sin_hi).astype(BF16)
    for j in range(N_PAIR):
        v_ref[j] = vv[:, j * LANES:(j + 1) * LANES].astype(BF16)


def _even_in_kernel(x_ref, mod_ref, g_ref, win_ref, qan_ref, wq_ref, kvan_ref, wk_ref, wv_ref, qg_ref, kg_ref,
                    cos_ref, sl_ref, sh_ref,
                    q_ref, k_ref, v_ref, ckv_ref, kr_ref, rq_ref, rk_ref, rv_ref, rg_ref):
    h = _prenorm(x_ref[...], g_ref[...], mod_ref[0, 0:1, :], mod_ref[0, 1:2, :])
    z = jnp.dot(h.astype(BF16), win_ref[...], preferred_element_type=F32)
    cq = z[:, 0:256]
    ckv = z[:, 256:384]
    kr_slab = z[:, 384:512]
    cos, sin_lo, sin_hi = cos_ref[...], sl_ref[...], sh_ref[...]

    qq = jnp.dot(_rms(cq, qan_ref[...]).astype(BF16), wq_ref[...], preferred_element_type=F32)
    qg = qg_ref[...]
    for hh in range(H_MLA):
        q_ref[hh] = _head_norm_rope(qq[:, hh * LANES:(hh + 1) * LANES], qg, cos, sin_lo, sin_hi).astype(BF16)

    ckvn = _rms(ckv, kvan_ref[...])
    ckv_ref[...] = ckvn
    kr_ref[...] = kr_slab
    _kv_heads(ckvn.astype(BF16), kr_slab, wk_ref, wv_ref, kg_ref[...], cos, sin_lo, sin_hi, k_ref, v_ref)

    rq_ref[...] = z[:, 512:768].astype(BF16)
    rk_ref[...] = (z[:, 768:1024] * (RET_DK ** -0.5)).astype(BF16)
    rv_ref[...] = z[:, 1024:1536].astype(BF16)
    rg_ref[...] = z[:, 1536:2048]


def _even_in(x, mod, gain, wts, rope, *, n_tiles, x_blk, mod_row, rope_blk, kv_rows, kv_blk):
    win, qan, wq, kvan, wk, wv, qg, kg = wts
    cos, sin_lo, sin_hi = rope
    ntok = n_tiles * TM
    row = lambda f: (lambda i: (f(i), 0))
    tab = pl.BlockSpec((TM, LANES), row(rope_blk))
    heads = lambda f: (lambda i: (0, f(i), 0))
    out_shape = (
        jax.ShapeDtypeStruct((H_MLA, ntok, LANES), BF16),
        jax.ShapeDtypeStruct((H_MLA, kv_rows, LANES), BF16),
        jax.ShapeDtypeStruct((N_PAIR, kv_rows, LANES), BF16),
        jax.ShapeDtypeStruct((ntok, KV_RANK), F32),
        jax.ShapeDtypeStruct((ntok, LANES), F32),
        jax.ShapeDtypeStruct((ntok, H_RET * RET_DK), BF16),
        jax.ShapeDtypeStruct((ntok, H_RET * RET_DK), BF16),
        jax.ShapeDtypeStruct((ntok, H_RET * RET_DV), BF16),
        jax.ShapeDtypeStruct((ntok, H_RET * RET_DV), F32),
    )
    ident = lambda i: i
    out_specs = (
        pl.BlockSpec((H_MLA, TM, LANES), heads(ident)),
        pl.BlockSpec((H_MLA, TM, LANES), heads(kv_blk)),
        pl.BlockSpec((N_PAIR, TM, LANES), heads(kv_blk)),
        pl.BlockSpec((TM, KV_RANK), row(ident)),
        pl.BlockSpec((TM, LANES), row(ident)),
        pl.BlockSpec((TM, H_RET * RET_DK), row(ident)),
        pl.BlockSpec((TM, H_RET * RET_DK), row(ident)),
        pl.BlockSpec((TM, H_RET * RET_DV), row(ident)),
        pl.BlockSpec((TM, H_RET * RET_DV), row(ident)),
    )
    return pl.pallas_call(
        _even_in_kernel,
        grid=(n_tiles,),
        in_specs=[pl.BlockSpec((TM, D_MODEL), row(x_blk)),
                  pl.BlockSpec((1, 6, D_MODEL), lambda i: (mod_row(i), 0, 0)),
                  _full(gain.shape), _full(win.shape), _full(qan.shape), _full(wq.shape), _full(kvan.shape),
                  _full(wk.shape), _full(wv.shape), _full(qg.shape), _full(kg.shape), tab, tab, tab],
        out_specs=out_specs,
        out_shape=out_shape,
        compiler_params=_cparams(("arbitrary",)),
        name="even_in_proj",
    )(x, mod, gain, win, qan, wq, kvan, wk, wv, qg, kg, cos, sin_lo, sin_hi)


def _ctx_kv_kernel(ckv_ref, kr_ref, wk_ref, wv_ref, kg_ref, cos_ref, sl_ref, sh_ref, k_in, v_in, k_ref, v_ref):
    del k_in, v_in
    _kv_heads(ckv_ref[...].astype(BF16), kr_ref[...], wk_ref, wv_ref, kg_ref[...],
              cos_ref[...], sl_ref[...], sh_ref[...], k_ref, v_ref)


def _ctx_kv(cache_ckv, cache_kr_slab, wk, wv, kg, rope_ident, k_all, v_all):
    cos, sin_lo, sin_hi = rope_ident
    nb = cache_ckv.shape[0] // PAST_LEN
    blk = lambda b: (0, b * KV_BLOCKS_PER_SEQ, 0)
    any_spec = pl.BlockSpec(memory_space=pl.ANY)
    return pl.pallas_call(
        _ctx_kv_kernel,
        grid=(nb,),
        in_specs=[pl.BlockSpec((PAST_LEN, KV_RANK), lambda b: (b, 0)),
                  pl.BlockSpec((PAST_LEN, LANES), lambda b: (b, 0)),
                  _full(wk.shape), _full(wv.shape), _full(kg.shape),
                  _full(cos.shape), _full(cos.shape), _full(cos.shape), any_spec, any_spec],
        out_specs=(pl.BlockSpec((H_MLA, PAST_LEN, LANES), blk), pl.BlockSpec((N_PAIR, PAST_LEN, LANES), blk)),
        out_shape=(jax.ShapeDtypeStruct(k_all.shape, k_all.dtype), jax.ShapeDtypeStruct(v_all.shape, v_all.dtype)),
        input_output_aliases={8: 0, 9: 1},
        compiler_params=_cparams(("arbitrary",)),
        name="ctx_kv_heads",
    )(cache_ckv, cache_kr_slab, wk, wv, kg, cos, sin_lo, sin_hi, k_all, v_all)


def _attn_kernel(q_ref, k_ref, v_ref, *rest):
    o_ref = rest[-1]
    v = v_ref[0]
    outs = []
    for a in range(2):
        s = lax.dot_general(q_ref[a], k_ref[a], (((1,), (1,)), ((), ())), preferred_element_type=F32)
        p = jnp.exp(s - jnp.max(s, axis=-1, keepdims=True))
        l = jnp.sum(p, axis=-1, keepdims=True)
        outs.append(jnp.dot(p.astype(BF16), v, preferred_element_type=F32) / l)
    lane = lax.broadcasted_iota(jnp.int32, outs[0].shape, 1)
    o_ref[0] = jnp.where(lane < MLA_V, outs[0], outs[1]).astype(BF16)


def _attention(q, k, v, o_prev, *, nb, nq, lk, o_blk0):
    in_specs = [pl.BlockSpec((2, TM, LANES), lambda b, p, i: (p, b * nq + i, 0)),
                pl.BlockSpec((2, lk, LANES), lambda b, p, i: (p, b, 0)),
                pl.BlockSpec((1, lk, LANES), lambda b, p, i: (p, b, 0))]
    args = [q, k, v]
    aliases = {}
    if o_prev is not None:
        in_specs.append(pl.BlockSpec(memory_space=pl.ANY))
        args.append(o_prev)
        aliases = {3: 0}
    return pl.pallas_call(
        _attn_kernel,
        grid=(nb, N_PAIR, nq),
        in_specs=in_specs,
        out_specs=pl.BlockSpec((1, TM, LANES), lambda b, p, i: (p, o_blk0 + b * nq + i, 0)),
        out_shape=jax.ShapeDtypeStruct((N_PAIR, NT, LANES), BF16),
        input_output_aliases=aliases,
        compiler_params=_cparams(("arbitrary", "arbitrary", "arbitrary")),
        name="mla_attention",
    )(*args)


def _ret_kernel(dec_ref, q_ref, k_ref, v_ref, s0_ref, *rest, reverse, finish):
    if finish:
        of_ref, g_ref, gn_ref = rest[:3]
    o_ref, st_ref, s_scr = rest[-3:]
    j = pl.program_id(1)
    d = 1 if reverse else 0
    C = RET_CHUNK

    @pl.when(j == 0)
    def _():
        s_scr[...] = s0_ref[0]

    row = lax.broadcasted_iota(jnp.int32, (C, C), 0).astype(F32)
    col = lax.broadcasted_iota(jnp.int32, (C, C), 1).astype(F32)
    dist = (col - row) if reverse else (row - col)
    live = dist >= 0.0
    rowv = lax.broadcasted_iota(jnp.int32, (C, RET_DV), 0).astype(F32)
    rowk = lax.broadcasted_iota(jnp.int32, (C, RET_DK), 0).astype(F32)
    for h in range(H_RET):
        dl = jnp.full((1, LANES), dec_ref[d, h], F32)
        lg = jnp.minimum(dl, 0.0) - jnp.log1p(jnp.exp(-jnp.abs(dl)))
        q = q_ref[:, h * RET_DK:(h + 1) * RET_DK]
        k = k_ref[:, h * RET_DK:(h + 1) * RET_DK]
        v = v_ref[:, h * RET_DV:(h + 1) * RET_DV]
        dmask = jnp.where(live, jnp.exp(jnp.where(live, dist, 0.0) * lg), 0.0)
        scores = lax.dot_general(q, k, (((1,), (1,)), ((), ())), preferred_element_type=F32) * dmask
        inner = jnp.dot(scores.astype(BF16), v, preferred_element_type=F32)
        state = s_scr[h]
        xi = jnp.exp(((C - rowv) if reverse else (rowv + 1.0)) * lg)
        cross = jnp.dot(q, state.astype(BF16), preferred_element_type=F32) * xi
        wk = jnp.exp((rowk if reverse else (C - 1.0 - rowk)) * lg[:, :RET_DK])
        kw = (k.astype(F32) * wk).astype(BF16)
        u = lax.dot_general(kw, v, (((0,), (0,)), ((), ())), preferred_element_type=F32)
        new_state = jnp.exp(C * lg) * state + u
        s_scr[h] = new_state
        st_ref[0, h] = new_state
        o = inner + cross
        if finish:
            o = o + of_ref[:, h * RET_DV:(h + 1) * RET_DV]
            mu = jnp.mean(o, axis=-1, keepdims=True)
            var = jnp.mean(jnp.square(o - mu), axis=-1, keepdims=True)
            o = (o - mu) * lax.rsqrt(var + EPS) * gn_ref[:, h * RET_DV:(h + 1) * RET_DV]
            o = _silu(g_ref[:, h * RET_DV:(h + 1) * RET_DV]) * o
        o_ref[:, h * RET_DV:(h + 1) * RET_DV] = o.astype(o_ref.dtype)


def _retention_pass(decay, rq, rk, rv, s0, *, nb, nc, reverse, o_fwd=None, gate=None, gn=None, o_prev=None, o_blk0=0):
    C = RET_CHUNK
    chunk = (lambda b, j: (b * nc + (nc - 1 - j), 0)) if reverse else (lambda b, j: (b * nc + j, 0))
    in_specs = [pl.BlockSpec(memory_space=pltpu.SMEM),
                pl.BlockSpec((C, H_RET * RET_DK), chunk),
                pl.BlockSpec((C, H_RET * RET_DK), chunk),
                pl.BlockSpec((C, H_RET * RET_DV), chunk),
                pl.BlockSpec((1, H_RET, RET_DK, RET_DV), lambda b, j: (b, 0, 0, 0))]
    args = [decay, rq, rk, rv, s0]
    finish = o_fwd is not None
    aliases = {}
    if finish:
        in_specs += [pl.BlockSpec((C, H_RET * RET_DV), chunk), pl.BlockSpec((C, H_RET * RET_DV), chunk),
                     _full(gn.shape)]
        args += [o_fwd, gate, gn]
        if o_prev is not None:
            in_specs.append(pl.BlockSpec(memory_space=pl.ANY))
            args.append(o_prev)
            aliases = {len(args) - 1: 0}
        o_spec = pl.BlockSpec((C, H_RET * RET_DV), lambda b, j: (o_blk0 + b * nc + (nc - 1 - j), 0))
        o_shape = jax.ShapeDtypeStruct((NT, H_RET * RET_DV), BF16)
    else:
        o_spec = pl.BlockSpec((C, H_RET * RET_DV), chunk)
        o_shape = jax.ShapeDtypeStruct((nb * nc * C, H_RET * RET_DV), F32)
    return pl.pallas_call(
        functools.partial(_ret_kernel, reverse=reverse, finish=finish),
        grid=(nb, nc),
        in_specs=in_specs,
        out_specs=(o_spec, pl.BlockSpec((1, H_RET, RET_DK, RET_DV), lambda b, j: (b, 0, 0, 0))),
        out_shape=(o_shape, jax.ShapeDtypeStruct((nb, H_RET, RET_DK, RET_DV), F32)),
        scratch_shapes=[pltpu.VMEM((H_RET, RET_DK, RET_DV), F32)],
        input_output_aliases=aliases,
        compiler_params=_cparams(("arbitrary", "arbitrary")),
        name="retention_bwd" if reverse else "retention_fwd",
    )(*args)


def _row_copy(src_ref, s, dst_ref, d, sem):
    return pltpu.make_async_copy(src_ref.at[pl.ds(s, 1), :], dst_ref.at[pl.ds(d, 1), :], sem)


def _wait_rows(src_ref, dst_ref, sem, n):
    def body(r, c):
        _row_copy(src_ref, 0, dst_ref, 0, sem).wait()
        return c
    lax.fori_loop(0, n, body, 0, unroll=ISSUE_UNROLL)


def _residual_and_route(x, y, mod_ref, gffn_ref, wrh_ref, wrl_ref, br_ref, x1_ref, route_ref, cnt_ref, xbuf_ref,
                        cnt_scr, stage_scr, slot_vm, slot_sm, row_sem, slot_sem):
    i = pl.program_id(0)
    n_steps = pl.num_programs(0)

    @pl.when(i == 0)
    def _():
        cnt_scr[...] = jnp.zeros_like(cnt_scr)

    x1 = x + mod_ref[0, 2:3, :] * y
    x1_ref[...] = x1
    h2 = _prenorm(x1, gffn_ref[...], mod_ref[0, 3:4, :], mod_ref[0, 4:5, :])
    hi = h2.astype(BF16)
    hi32 = hi.astype(F32)
    lo = (h2 - hi32).astype(BF16)
    wrh = wrh_ref[...]
    lg = (jnp.dot(hi, wrh, preferred_element_type=F32) + jnp.dot(lo, wrh, preferred_element_type=F32)
          + jnp.dot(hi, wrl_ref[...], preferred_element_type=F32) + br_ref[...])
    lane = lax.broadcasted_iota(jnp.int32, lg.shape, 1).astype(F32)
    big = float(4 * LANES)
    gl = jnp.where((lane >= N_EXPERTS) & (lane < N_EXPERTS + N_GROUPS), lg, NEG)
    gmax = jnp.max(gl, axis=-1, keepdims=True)
    g_w = 1.0 / jnp.sum(jnp.exp(gl - gmax), axis=-1, keepdims=True)
    g_lane = jnp.min(jnp.where(gl == gmax, lane, big), axis=-1, keepdims=True)
    e_lo = (g_lane - N_EXPERTS) * EXP_PER_GROUP
    el = jnp.where((lane >= e_lo) & (lane < e_lo + EXP_PER_GROUP), lg, NEG)
    m1 = jnp.max(el, axis=-1, keepdims=True)
    i1 = jnp.min(jnp.where(el == m1, lane, big), axis=-1, keepdims=True)
    el2 = jnp.where(lane == i1, NEG, el)
    m2 = jnp.max(el2, axis=-1, keepdims=True)
    i2 = jnp.min(jnp.where(el2 == m2, lane, big), axis=-1, keepdims=True)
    t = jnp.exp(m2 - m1)
    w1 = g_w / (1.0 + t)
    w2 = w1 * t
    oh1 = lane == i1
    oh2 = lane == i2
    oh = jnp.where(oh1 | oh2, 1.0, 0.0)
    rr = lax.broadcasted_iota(jnp.int32, (TM, TM), 0)
    cc = lax.broadcasted_iota(jnp.int32, (TM, TM), 1)
    strict_lower = jnp.where(cc < rr, 1.0, 0.0).astype(BF16)
    before = jnp.dot(strict_lower, oh.astype(BF16), preferred_element_type=F32) + cnt_scr[...]
    slot1 = jnp.sum(jnp.where(oh1, before, 0.0), axis=-1, keepdims=True) + i1 * CAP
    slot2 = jnp.sum(jnp.where(oh2, before, 0.0), axis=-1, keepdims=True) + i2 * CAP
    cnt = cnt_scr[...] + jnp.sum(oh, axis=0, keepdims=True)
    cnt_scr[...] = cnt
    cnt_ref[...] = cnt
    route = jnp.where(lane == 0.0, i1, jnp.where(lane == 1.0, i2, jnp.where(lane == 2.0, w1, jnp.where(
        lane == 3.0, w2, jnp.where(lane == 4.0, slot1, jnp.where(lane == 5.0, slot2, 0.0))))))
    route_ref[...] = route

    @pl.when(i > 0)
    def _():
        _wait_rows(stage_scr, xbuf_ref, row_sem, 2 * TM)

    stage_scr[...] = (lax.bitcast_convert_type(hi32[:, :PACKED], U32)
                      | (lax.bitcast_convert_type(hi32[:, PACKED:], U32) >> 16))
    slot_vm[...] = jnp.transpose(route)[0:SUBLANES, :].astype(jnp.int32)
    to_smem = pltpu.make_async_copy(slot_vm, slot_sm, slot_sem)
    to_smem.start()
    to_smem.wait()

    def issue(r, c):
        _row_copy(stage_scr, r, xbuf_ref, slot_sm[4, r], row_sem).start()
        _row_copy(stage_scr, r, xbuf_ref, slot_sm[5, r], row_sem).start()
        return c
    lax.fori_loop(0, TM, issue, 0, unroll=ISSUE_UNROLL)

    @pl.when(i == n_steps - 1)
    def _():
        _wait_rows(stage_scr, xbuf_ref, row_sem, 2 * TM)


def _tail_specs():
    row = lambda i: (i, 0)
    in_specs = [pl.BlockSpec((1, 6, D_MODEL), lambda i: (_cond_row(i), 0, 0)),
                _full((1, D_MODEL)), _full((D_MODEL, LANES)), _full((D_MODEL, LANES)), _full((1, LANES))]
    out_specs = (pl.BlockSpec((TM, D_MODEL), row), pl.BlockSpec((TM, LANES), row), _full((1, LANES)),
                 pl.BlockSpec(memory_space=pl.ANY))
    out_shape = (jax.ShapeDtypeStruct((NT, D_MODEL), F32), jax.ShapeDtypeStruct((NT, LANES), F32),
                 jax.ShapeDtypeStruct((1, LANES), F32), jax.ShapeDtypeStruct((N_EXPERTS * CAP, PACKED), U32))
    scratch = [pltpu.VMEM((1, LANES), F32), pltpu.VMEM((TM, PACKED), U32),
               pltpu.VMEM((SUBLANES, TM), jnp.int32), pltpu.SMEM((SUBLANES, TM), jnp.int32),
               pltpu.SemaphoreType.DMA(()), pltpu.SemaphoreType.DMA(())]
    return in_specs, out_specs, out_shape, scratch


def _cond_row(i):
    return jnp.where(i < NP_TILES, DEC_BATCH, (i - NP_TILES) // S_TILES_PER_SEQ)


def _even_out_kernel(x_ref, o_ref, r_ref, wout_ref, *tail):
    y = jnp.dot(r_ref[...], wout_ref[H_MLA * MLA_V:, :], preferred_element_type=F32)
    for p in range(N_PAIR):
        y = y + jnp.dot(o_ref[p], wout_ref[p * LANES:(p + 1) * LANES, :], preferred_element_type=F32)
    _residual_and_route(x_ref[...], y, *tail)


def _even_out(x, o_mla, o_ret, wout, mod, gffn, wrh, wrl, br):
    row = lambda i: (i, 0)
    tail_in, out_specs, out_shape, scratch = _tail_specs()
    return pl.pallas_call(
        _even_out_kernel,
        grid=(NT_TILES,),
        in_specs=[pl.BlockSpec((TM, D_MODEL), row),
                  pl.BlockSpec((N_PAIR, TM, LANES), lambda i: (0, i, 0)),
                  pl.BlockSpec((TM, H_RET * RET_DV), row),
                  _full(wout.shape)] + tail_in,
        out_specs=out_specs,
        out_shape=out_shape,
        scratch_shapes=scratch,
        compiler_params=_cparams(("arbitrary",)),
        name="even_out_route",
    )(x, o_mla, o_ret, wout, mod, gffn, wrh, wrl, br)


def _conf_in_kernel(x_ref, mod_ref, g_ref, win_ref, u_ref):
    h = _prenorm(x_ref[...], g_ref[...], mod_ref[0, 0:1, :], mod_ref[0, 1:2, :])
    z = jnp.dot(h.astype(BF16), win_ref[...], preferred_element_type=F32)
    u_ref[...] = z[:, :CONV_DIM] * jax.nn.sigmoid(z[:, CONV_DIM:])


def _conf_in(x, mod, gain, win):
    row = lambda i: (i, 0)
    return pl.pallas_call(
        _conf_in_kernel,
        grid=(NT_TILES,),
        in_specs=[pl.BlockSpec((TM, D_MODEL), row),
                  pl.BlockSpec((1, 6, D_MODEL), lambda i: (_cond_row(i), 0, 0)),
                  _full(gain.shape), _full(win.shape)],
        out_specs=pl.BlockSpec((TM, CONV_DIM), row),
        out_shape=jax.ShapeDtypeStruct((NT, CONV_DIM), F32),
        compiler_params=_cparams(("arbitrary",)),
        name="conformer_in_glu",
    )(x, mod, gain, win)


CONV_ROWS = 32
SHIFT_ROWS = TM + 2 * HALO - SUBLANES


def _conv_out_kernel(x_ref, u_ref, ul_ref, ur_ref, dw_ref, dwb_ref, lng_ref, lnb_ref, wout_ref, *tail_and_scratch):
    tail = tail_and_scratch[:-3]
    ext_scr, shift_scr, act_scr = tail_and_scratch[-3:]
    i = pl.program_id(0)
    t = (i - NP_TILES) % S_TILES_PER_SEQ
    first = (i < NP_TILES) | (t == 0)
    last = (i < NP_TILES) | (t == S_TILES_PER_SEQ - 1)
    ext_scr[0:HALO, :] = jnp.where(first, 0.0, ul_ref[...])
    ext_scr[HALO:HALO + TM, :] = u_ref[...]
    ext_scr[HALO + TM:, :] = jnp.where(last, 0.0, ur_ref[...])
    for s in range(SUBLANES):
        shift_scr[s] = ext_scr[pl.ds(s, SHIFT_ROWS), :]
    off = HALO - CONV_W // 2
    for rb in range(TM // CONV_ROWS):
        acc = jnp.zeros((CONV_ROWS, CONV_DIM), F32)
        for kk in range(CONV_W):
            s, a = (off + kk) % SUBLANES, (off + kk) // SUBLANES * SUBLANES
            acc = acc + shift_scr[s, pl.ds(rb * CONV_ROWS + a, CONV_ROWS), :] * dw_ref[kk:kk + 1, :]
        c = acc + dwb_ref[...]
        mu = jnp.mean(c, axis=-1, keepdims=True)
        var = jnp.mean(jnp.square(c - mu), axis=-1, keepdims=True)
        c = (c - mu) * lax.rsqrt(var + EPS) * lng_ref[...] + lnb_ref[...]
        act_scr[rb * CONV_ROWS:(rb + 1) * CONV_ROWS, :] = _silu(c).astype(BF16)
    y = jnp.dot(act_scr[...], wout_ref[...], preferred_element_type=F32)
    _residual_and_route(x_ref[...], y, *tail)


def _conv_out(x, u, dw, dwb, lng, lnb, wout, mod, gffn, wrh, wrl, br):
    row = lambda i: (i, 0)
    per = TM // HALO
    n_halo = NT // HALO
    tail_in, out_specs, out_shape, scratch = _tail_specs()
    return pl.pallas_call(
        _conv_out_kernel,
        grid=(NT_TILES,),
        in_specs=[pl.BlockSpec((TM, D_MODEL), row),
                  pl.BlockSpec((TM, CONV_DIM), row),
                  pl.BlockSpec((HALO, CONV_DIM), lambda i: (jnp.maximum(i * per - 1, 0), 0)),
                  pl.BlockSpec((HALO, CONV_DIM), lambda i: (jnp.minimum((i + 1) * per, n_halo - 1), 0)),
                  _full(dw.shape), _full(dwb.shape), _full(lng.shape), _full(lnb.shape), _full(wout.shape)] + tail_in,
        out_specs=out_specs,
        out_shape=out_shape,
        scratch_shapes=scratch + [pltpu.VMEM((TM + 2 * HALO, CONV_DIM), F32),
                                  pltpu.VMEM((SUBLANES, SHIFT_ROWS, CONV_DIM), F32),
                                  pltpu.VMEM((TM, CONV_DIM), BF16)],
        compiler_params=_cparams(("arbitrary",)),
        name="conformer_conv_out_route",
    )(x, u, u, u, dw, dwb, lng, lnb, wout, mod, gffn, wrh, wrl, br)


def _pad_fill_kernel(cnt_ref, xin_ref, xbuf_ref, zero_scr, sem):
    del xin_ref
    e = pl.program_id(0)
    n = cnt_ref[e]
    pad = (EB - n % EB) % EB
    zero_scr[...] = jnp.zeros_like(zero_scr)

    def start(r, c):
        @pl.when(r < pad)
        def _():
            _row_copy(zero_scr, 0, xbuf_ref, e * CAP + n + r, sem).start()
        return c

    def wait(r, c):
        @pl.when(r < pad)
        def _():
            _row_copy(zero_scr, 0, xbuf_ref, e * CAP + n + r, sem).wait()
        return c

    lax.fori_loop(0, EB - 1, start, 0)
    lax.fori_loop(0, EB - 1, wait, 0)


def _pad_fill(counts, xbuf):
    return pl.pallas_call(
        _pad_fill_kernel,
        grid_spec=pltpu.PrefetchScalarGridSpec(
            num_scalar_prefetch=1,
            grid=(N_EXPERTS,),
            in_specs=[pl.BlockSpec(memory_space=pl.ANY)],
            out_specs=pl.BlockSpec(memory_space=pl.ANY),
            scratch_shapes=[pltpu.VMEM((SUBLANES, PACKED), U32), pltpu.SemaphoreType.DMA(())]),
        out_shape=jax.ShapeDtypeStruct(xbuf.shape, xbuf.dtype),
        input_output_aliases={1: 0},
        compiler_params=_cparams(("arbitrary",)),
        name="moe_pad_fill",
    )(counts, xbuf)


def _expert_kernel(blk_ref, exp_ref, flag_ref, x_ref, wu_ref, wd_ref, o_ref, wu_scr, wd_scr):
    i = pl.program_id(0)
    flag = flag_ref[i]

    @pl.when((flag & 2) != 0)
    def _():
        wu_scr[...] = wu_ref[0].astype(BF16)
        wd_scr[...] = wd_ref[0].astype(BF16)

    @pl.when((flag & 1) != 0)
    def _():
        w = x_ref[...]
        xa = lax.bitcast_convert_type(w & jnp.uint32(0xFFFF0000), F32).astype(BF16)
        xb = lax.bitcast_convert_type(w << 16, F32).astype(BF16)
        ab = (jnp.dot(xa, wu_scr[:PACKED, :], preferred_element_type=F32)
              + jnp.dot(xb, wu_scr[PACKED:, :], preferred_element_type=F32))
        mid = (_silu(ab[:, :D_EXPERT]) * ab[:, D_EXPERT:]).astype(BF16)
        o_ref[...] = jnp.dot(mid, wd_scr[...], preferred_element_type=F32)


def _experts(blk, exp, flag, xbuf, w_up, w_down):
    return pl.pallas_call(
        _expert_kernel,
        grid_spec=pltpu.PrefetchScalarGridSpec(
            num_scalar_prefetch=3,
            grid=(N_ITEMS,),
            in_specs=[pl.BlockSpec((EB, PACKED), lambda i, b, e, f: (b[i], 0)),
                      pl.BlockSpec((1, D_MODEL, 2 * D_EXPERT), lambda i, b, e, f: (e[i], 0, 0)),
                      pl.BlockSpec((1, D_EXPERT, D_MODEL), lambda i, b, e, f: (e[i], 0, 0))],
            out_specs=pl.BlockSpec((EB, D_MODEL), lambda i, b, e, f: (b[i], 0)),
            scratch_shapes=[pltpu.VMEM((D_MODEL, 2 * D_EXPERT), BF16), pltpu.VMEM((D_EXPERT, D_MODEL), BF16)]),
        out_shape=jax.ShapeDtypeStruct((N_EXPERTS * CAP, D_MODEL), F32),
        compiler_params=_cparams(("arbitrary",)),
        name="moe_experts",
    )(blk, exp, flag, xbuf, w_up, w_down)


def _combine_kernel(slot_ref, x_ref, route_ref, mod_ref, ybuf_ref, o_ref, g_scr, sem):
    i = pl.program_id(0)
    n_steps = pl.num_programs(0)

    def gather(tile, buf):
        def issue(r, c):
            base = tile * (2 * TM) + 2 * r
            _row_copy(ybuf_ref, slot_ref[base], g_scr.at[buf], r, sem.at[buf]).start()
            _row_copy(ybuf_ref, slot_ref[base + 1], g_scr.at[buf], TM + r, sem.at[buf]).start()
            return c
        lax.fori_loop(0, TM, issue, 0, unroll=ISSUE_UNROLL)

    cur = i % 2

    @pl.when(i == 0)
    def _():
        gather(0, 0)

    @pl.when(i + 1 < n_steps)
    def _():
        gather(i + 1, 1 - cur)

    _wait_rows(ybuf_ref, g_scr.at[cur], sem.at[cur], 2 * TM)
    route = route_ref[...]
    y = route[:, 2:3] * g_scr[cur, 0:TM, :] + route[:, 3:4] * g_scr[cur, TM:2 * TM, :]
    o_ref[...] = x_ref[...] + mod_ref[0, 5:6, :] * y


def _combine(slots, x1, route, mod, ybuf):
    row = lambda i, s: (i, 0)
    return pl.pallas_call(
        _combine_kernel,
        grid_spec=pltpu.PrefetchScalarGridSpec(
            num_scalar_prefetch=1,
            grid=(NT_TILES,),
            in_specs=[pl.BlockSpec((TM, D_MODEL), row),
                      pl.BlockSpec((TM, LANES), row),
                      pl.BlockSpec((1, 6, D_MODEL), lambda i, s: (_cond_row(i), 0, 0)),
                      pl.BlockSpec(memory_space=pl.ANY)],
            out_specs=pl.BlockSpec((TM, D_MODEL), row),
            scratch_shapes=[pltpu.VMEM((2, 2 * TM, D_MODEL), F32), pltpu.SemaphoreType.DMA((2,))]),
        out_shape=jax.ShapeDtypeStruct((NT, D_MODEL), F32),
        compiler_params=_cparams(("arbitrary",)),
        name="moe_combine",
    )(slots, x1, route, mod, ybuf)


def _moe(x1, xbuf, route, counts, mod, w_up, w_down):
    slots = route[:, 4:6].astype(jnp.int32).reshape(-1)
    cnt = counts[0, :N_EXPERTS].astype(jnp.int32)
    nblk = (cnt + EB - 1) // EB
    ends = jnp.cumsum(nblk)
    total = ends[-1]
    item = jnp.arange(N_ITEMS, dtype=jnp.int32)
    valid = item < total
    item_c = jnp.minimum(item, total - 1)
    exp = jnp.minimum(jnp.sum((item_c[:, None] >= ends[None, :]).astype(jnp.int32), axis=1), N_EXPERTS - 1)
    j = item_c - (ends[exp] - nblk[exp])
    blk = (exp * CAP_BLOCKS + j).astype(jnp.int32)
    flag = (valid.astype(jnp.int32) + 2 * (valid & (j == 0)).astype(jnp.int32)).astype(jnp.int32)

    xbuf = _pad_fill(cnt, xbuf)
    ybuf = _experts(blk, exp, flag, xbuf, w_up, w_down)
    return _combine(slots, x1, route, mod, ybuf)


def _pad_heads(w, width, real):
    lead = w.shape[:-1]
    w = w.reshape(lead + (H_MLA, real))
    w = jnp.pad(w, [(0, 0)] * len(lead) + [(0, 0), (0, width - real)])
    return w.reshape(lead + (H_MLA * width,))


def _rope_tables():
    L = DEC_SEQ
    rows = L // GRID_W
    r = jnp.repeat(jnp.arange(rows, dtype=F32), GRID_W)
    col = jnp.tile(jnp.arange(GRID_W, dtype=F32), rows)
    n_f = ROPE_DIM // 4
    freqs = ROPE_BASE ** (-jnp.arange(n_f, dtype=F32) / n_f)
    ang = jnp.concatenate([r[:, None] * freqs, col[:, None] * freqs], axis=-1)
    cos, sin = jnp.cos(ang), jnp.sin(ang)
    half = ROPE_DIM // 2
    z = lambda n: jnp.zeros((L, n), F32)
    o = lambda n: jnp.ones((L, n), F32)
    cos_t = jnp.concatenate([o(MLA_NOPE), cos, cos, o(LANES - MLA_QK)], axis=-1)
    sin_lo = jnp.concatenate([z(MLA_NOPE), -sin, z(half), z(LANES - MLA_QK)], axis=-1)
    sin_hi = jnp.concatenate([z(MLA_NOPE), z(half), sin, z(LANES - MLA_QK)], axis=-1)
    ident = (jnp.ones((TM, LANES), F32), jnp.zeros((TM, LANES), F32), jnp.zeros((TM, LANES), F32))
    return (cos_t, sin_lo, sin_hi), ident


def _even_weights(w_in, q_a_norm, w_q_up, kv_a_norm, w_kv_up, q_norm, k_norm):
    d = D_MODEL
    kr_cols = jnp.zeros((d, LANES), F32).at[:, MLA_NOPE:MLA_QK].set(w_in[:, Q_RANK + KV_RANK:Q_RANK + KV_RANK + ROPE_DIM])
    rest = w_in[:, Q_RANK + KV_RANK + ROPE_DIM:]
    win = jnp.concatenate([w_in[:, :Q_RANK + KV_RANK], kr_cols, rest], axis=-1).astype(BF16)
    wq = _pad_heads(w_q_up, LANES, MLA_QK).astype(BF16)
    kv = w_kv_up.reshape(KV_RANK, H_MLA, MLA_NOPE + MLA_V)
    wk = _pad_heads(kv[:, :, :MLA_NOPE].reshape(KV_RANK, H_MLA * MLA_NOPE), LANES, MLA_NOPE).astype(BF16)
    wv = kv[:, :, MLA_NOPE:].reshape(KV_RANK, H_MLA * MLA_V).astype(BF16)
    padg = lambda g: jnp.pad(g, (0, LANES - MLA_QK)).reshape(1, LANES)
    qg = padg(q_norm) * (MLA_QK ** -0.5)
    kg = padg(k_norm)
    return (win, q_a_norm.reshape(1, -1), wq, kv_a_norm.reshape(1, -1), wk, wv, qg, kg)


def _router_weights(w_group, b_group, w_expert, b_expert):
    w = jnp.zeros((D_MODEL, LANES), F32).at[:, :N_EXPERTS].set(w_expert).at[:, N_EXPERTS:N_EXPERTS + N_GROUPS].set(w_group)
    b = jnp.zeros((1, LANES), F32).at[0, :N_EXPERTS].set(b_expert).at[0, N_EXPERTS:N_EXPERTS + N_GROUPS].set(b_group)
    hi = w.astype(BF16)
    lo = (w - hi.astype(F32)).astype(BF16)
    return hi, lo, b


def kernel(x_prompt, x_sample, cache_mla_ckv, cache_mla_krope, state_retention, c, c_ctx, ada_w, ada_b, norm_mix, norm_ffn, ev_w_in, ev_q_a_norm, ev_w_q_up, ev_kv_a_norm, ev_w_kv_up, ev_q_norm, ev_k_norm, ev_ret_decay, ev_ret_gn, ev_w_out, od_w_in, od_dw, od_dw_b, od_ln_g, od_ln_b, od_w_out, moe_w_group, moe_b_group, moe_w_expert, moe_b_expert, moe_w_up, moe_w_down):
    depth = ada_w.shape[0]
    x = jnp.concatenate([x_prompt.reshape(NP, D_MODEL), x_sample.reshape(NS, D_MODEL)], axis=0)
    cond = jnp.concatenate([c, c_ctx[None, :], jnp.zeros((2 * SUBLANES - DEC_BATCH - 1, D_MODEL), F32)], axis=0)
    mods = _ada(cond, ada_w, ada_b).reshape(depth, cond.shape[0], 6, D_MODEL)
    rope, rope_ident = _rope_tables()
    new_ckv, new_krope, new_state = [], [], []

    for l in range(depth):
        mod = mods[l]
        jj = l // 2
        router = _router_weights(moe_w_group[l], moe_b_group[l], moe_w_expert[l], moe_b_expert[l])
        gmix = norm_mix[l].reshape(1, D_MODEL)
        gffn = norm_ffn[l].reshape(1, D_MODEL)
        if l % 2 == 0:
            wts = _even_weights(ev_w_in[jj], ev_q_a_norm[jj], ev_w_q_up[jj], ev_kv_a_norm[jj], ev_w_kv_up[jj],
                                ev_q_norm[jj], ev_k_norm[jj])
            qp, kp, vp, ckv_p, kr_p, rq_p, rk_p, rv_p, rg_p = _even_in(
                x, mod, gmix, wts, rope_ident, n_tiles=NP_TILES, x_blk=lambda i: i, mod_row=lambda i: DEC_BATCH,
                rope_blk=lambda i: 0, kv_rows=NP, kv_blk=lambda i: i)
            qs, ks, vs, _, _, rq_s, rk_s, rv_s, rg_s = _even_in(
                x, mod, gmix, wts, rope, n_tiles=NS // TM, x_blk=lambda i: NP_TILES + i,
                mod_row=lambda i: i // S_TILES_PER_SEQ, rope_blk=lambda i: i % S_TILES_PER_SEQ,
                kv_rows=DEC_BATCH * LK_S,
                kv_blk=lambda i: (i // S_TILES_PER_SEQ) * KV_BLOCKS_PER_SEQ + 1 + i % S_TILES_PER_SEQ)
            kr_cache = jnp.pad(cache_mla_krope[:, jj].reshape(DEC_BATCH * PAST_LEN, ROPE_DIM),
                               ((0, 0), (MLA_NOPE, LANES - MLA_QK)))
            ks, vs = _ctx_kv(cache_mla_ckv[:, jj].reshape(DEC_BATCH * PAST_LEN, KV_RANK), kr_cache,
                             wts[4], wts[5], wts[7], rope_ident, ks, vs)
            o_mla = _attention(qp, kp, vp, None, nb=BATCH, nq=1, lk=SEQ, o_blk0=0)
            o_mla = _attention(qs, ks, vs, o_mla, nb=DEC_BATCH, nq=S_TILES_PER_SEQ, lk=LK_S, o_blk0=NP_TILES)

            decay = ev_ret_decay[jj]
            gn = ev_ret_gn[jj].reshape(1, -1)
            zero_state = jnp.zeros((BATCH, H_RET, RET_DK, RET_DV), F32)
            ncp, ncs = SEQ // RET_CHUNK, DEC_SEQ // RET_CHUNK
            of_p, sf_p = _retention_pass(decay, rq_p, rk_p, rv_p, zero_state, nb=BATCH, nc=ncp, reverse=False)
            o_ret, sb_p = _retention_pass(decay, rq_p, rk_p, rv_p, zero_state, nb=BATCH, nc=ncp, reverse=True,
                                          o_fwd=of_p, gate=rg_p, gn=gn)
            of_s, _ = _retention_pass(decay, rq_s, rk_s, rv_s, state_retention[:, jj, 0], nb=DEC_BATCH, nc=ncs,
                                      reverse=False)
            o_ret, _ = _retention_pass(decay, rq_s, rk_s, rv_s, state_retention[:, jj, 1], nb=DEC_BATCH, nc=ncs,
                                       reverse=True, o_fwd=of_s, gate=rg_s, gn=gn, o_prev=o_ret,
                                       o_blk0=NP // RET_CHUNK)
            new_ckv.append(ckv_p.reshape(BATCH, SEQ, KV_RANK))
            new_krope.append(kr_p[:, MLA_NOPE:MLA_QK].reshape(BATCH, SEQ, ROPE_DIM))
            new_state.append(jnp.stack([sf_p, sb_p], axis=1))
            x1, route, counts, xbuf = _even_out(x, o_mla, o_ret, ev_w_out[jj].astype(BF16), mod, gffn, *router)
        else:
            u = _conf_in(x, mod, gmix, od_w_in[jj].astype(BF16))
            x1, route, counts, xbuf = _conv_out(
                x, u, od_dw[jj], od_dw_b[jj].reshape(1, -1), od_ln_g[jj].reshape(1, -1), od_ln_b[jj].reshape(1, -1),
                od_w_out[jj].astype(BF16), mod, gffn, *router)
        x = _moe(x1, xbuf, route, counts, mod, moe_w_up[l], moe_w_down[l])

    y_prompt = x[:NP].reshape(BATCH, SEQ, D_MODEL)
    y_sample = x[NP:].reshape(DEC_BATCH, DEC_SEQ, D_MODEL)
    return (y_prompt, y_sample, jnp.stack(new_ckv, axis=1), jnp.stack(new_krope, axis=1),
            jnp.stack(new_state, axis=1))
```

```python
import functools

import jax
import jax.numpy as jnp
from jax import lax
from jax.experimental import pallas as pl
from jax.experimental.pallas import tpu as pltpu

F32 = jnp.float32
BF16 = jnp.bfloat16
U32 = jnp.uint32

D_MODEL = 1024
BATCH, SEQ = 32, 256
DEC_BATCH, DEC_SEQ = 8, 4096
PAST_LEN = 256
GRID_W = 64
EPS = 1e-6
H_MLA, Q_RANK, KV_RANK = 8, 256, 128
MLA_NOPE, ROPE_DIM, MLA_V = 64, 32, 64
MLA_QK = MLA_NOPE + ROPE_DIM
ROPE_BASE = 10000.0
H_RET, RET_DK, RET_DV, RET_CHUNK = 4, 64, 128, 128
CONV_DIM, CONV_W = 1024, 31
N_GROUPS, EXP_PER_GROUP, D_EXPERT = 4, 8, 256
N_EXPERTS = N_GROUPS * EXP_PER_GROUP

LANES = 128
SUBLANES = 8
VMEM_LIMIT = 48 * 1024 * 1024

NP = BATCH * SEQ
NS = DEC_BATCH * DEC_SEQ
NT = NP + NS
TM = 256
NT_TILES = NT // TM
NP_TILES = NP // TM
S_TILES_PER_SEQ = DEC_SEQ // TM
LK_S = PAST_LEN + DEC_SEQ
KV_BLOCKS_PER_SEQ = LK_S // TM
N_PAIR = H_MLA // 2
TQ_LATENT = 256
EB = 256
CAP = NT + EB
CAP_BLOCKS = CAP // EB
N_ITEMS = (2 * NT) // EB + N_EXPERTS
HALO = 16
PACKED = D_MODEL // 2
ISSUE_UNROLL = 8
NEG = -1e30


def _cparams(sem):
    return pltpu.CompilerParams(dimension_semantics=sem, vmem_limit_bytes=VMEM_LIMIT)


def _full(shape):
    n = len(shape)
    return pl.BlockSpec(shape, lambda *_: (0,) * n)


def _rms(x, gain):
    return x * lax.rsqrt(jnp.mean(x * x, axis=-1, keepdims=True) + EPS) * gain


def _prenorm(x, gain, shift, scale):
    return _rms(x, gain) * (1.0 + scale) + shift


def _silu(x):
    return x * jax.nn.sigmoid(x)


def _rope128(x, cos, sin_lo, sin_hi):
    return x * cos + pltpu.roll(x, LANES - ROPE_DIM // 2, 1) * sin_lo + pltpu.roll(x, ROPE_DIM // 2, 1) * sin_hi


def _head_norm_rope(slab, gain, cos, sin_lo, sin_hi):
    r = lax.rsqrt(jnp.sum(slab * slab, axis=-1, keepdims=True) * (1.0 / MLA_QK) + EPS)
    return _rope128(slab * r * gain, cos, sin_lo, sin_hi)


def _ada_kernel(c_ref, w_ref, b_ref, o_ref):
    s = _silu(c_ref[...]).astype(BF16)
    o_ref[0] = jnp.dot(s, w_ref[0].astype(BF16), preferred_element_type=F32) + b_ref[0]


def _ada(cond, ada_w, ada_b):
    depth, d, n = ada_w.shape
    rows = cond.shape[0]
    tn = 1536
    return pl.pallas_call(
        _ada_kernel,
        grid=(depth, n // tn),
        in_specs=[pl.BlockSpec((rows, d), lambda l, j: (0, 0)),
                  pl.BlockSpec((1, d, tn), lambda l, j: (l, 0, j)),
                  pl.BlockSpec((1, 1, tn), lambda l, j: (l, 0, j))],
        out_specs=pl.BlockSpec((1, rows, tn), lambda l, j: (l, 0, j)),
        out_shape=jax.ShapeDtypeStruct((depth, rows, n), F32),
        compiler_params=_cparams(("arbitrary", "arbitrary")),
        name="ada_modulation",
    )(cond, ada_w, ada_b.reshape(depth, 1, n))


def _kv_heads(ckvn_bf, kr_slab, wk_ref, wv_ref, kg, cos, sin_lo, sin_hi, k_ref, v_ref):
    kk = jnp.dot(ckvn_bf, wk_ref[...], preferred_element_type=F32)
    vv = jnp.dot(ckvn_bf, wv_ref[...], preferred_element_type=F32)
    for h in range(H_MLA):
        kh = kk[:, h * LANES:(h + 1) * LANES] + kr_slab
        k_ref[h] = _head_norm_rope(kh, kg, cos, sin_lo, sin_hi).astype(BF16)
    for j in range(N_PAIR):
        v_ref[j] = vv[:, j * LANES:(j + 1) * LANES].astype(BF16)


def _even_in_kernel(x_ref, mod_ref, g_ref, win_ref, qan_ref, wq_ref, kvan_ref, wk_ref, wv_ref, qg_ref, kg_ref,
                    cos_ref, sl_ref, sh_ref,
                    q_ref, k_ref, v_ref, ckv_ref, kr_ref, rq_ref, rk_ref, rv_ref, rg_ref):
    h = _prenorm(x_ref[...], g_ref[...], mod_ref[0, 0:1, :], mod_ref[0, 1:2, :])
    z = jnp.dot(h.astype(BF16), win_ref[...], preferred_element_type=F32)
    cq = z[:, 0:256]
    ckv = z[:, 256:384]
    kr_slab = z[:, 384:512]
    cos, sin_lo, sin_hi = cos_ref[...], sl_ref[...], sh_ref[...]

    qq = jnp.dot(_rms(cq, qan_ref[...]).astype(BF16), wq_ref[...], preferred_element_type=F32)
    qg = qg_ref[...]
    for hh in range(H_MLA):
        q_ref[hh] = _head_norm_rope(qq[:, hh * LANES:(hh + 1) * LANES], qg, cos, sin_lo, sin_hi).astype(BF16)

    ckvn = _rms(ckv, kvan_ref[...])
    ckv_ref[...] = ckvn
    kr_ref[...] = kr_slab
    _kv_heads(ckvn.astype(BF16), kr_slab, wk_ref, wv_ref, kg_ref[...], cos, sin_lo, sin_hi, k_ref, v_ref)

    rq_ref[...] = z[:, 512:768].astype(BF16)
    rk_ref[...] = (z[:, 768:1024] * (RET_DK ** -0.5)).astype(BF16)
    rv_ref[...] = z[:, 1024:1536].astype(BF16)
    rg_ref[...] = z[:, 1536:2048]


def _even_in(x, mod, gain, wts, rope, *, n_tiles, x_blk, mod_row, rope_blk, kv_rows, kv_blk):
    win, qan, wq, kvan, wk, wv, qg, kg = wts
    cos, sin_lo, sin_hi = rope
    ntok = n_tiles * TM
    row = lambda f: (lambda i: (f(i), 0))
    tab = pl.BlockSpec((TM, LANES), row(rope_blk))
    heads = lambda f: (lambda i: (0, f(i), 0))
    out_shape = (
        jax.ShapeDtypeStruct((H_MLA, ntok, LANES), BF16),
        jax.ShapeDtypeStruct((H_MLA, kv_rows, LANES), BF16),
        jax.ShapeDtypeStruct((N_PAIR, kv_rows, LANES), BF16),
        jax.ShapeDtypeStruct((ntok, KV_RANK), F32),
        jax.ShapeDtypeStruct((ntok, LANES), F32),
        jax.ShapeDtypeStruct((ntok, H_RET * RET_DK), BF16),
        jax.ShapeDtypeStruct((ntok, H_RET * RET_DK), BF16),
        jax.ShapeDtypeStruct((ntok, H_RET * RET_DV), BF16),
        jax.ShapeDtypeStruct((ntok, H_RET * RET_DV), F32),
    )
    ident = lambda i: i
    out_specs = (
        pl.BlockSpec((H_MLA, TM, LANES), heads(ident)),
        pl.BlockSpec((H_MLA, TM, LANES), heads(kv_blk)),
        pl.BlockSpec((N_PAIR, TM, LANES), heads(kv_blk)),
        pl.BlockSpec((TM, KV_RANK), row(ident)),
        pl.BlockSpec((TM, LANES), row(ident)),
        pl.BlockSpec((TM, H_RET * RET_DK), row(ident)),
        pl.BlockSpec((TM, H_RET * RET_DK), row(ident)),
        pl.BlockSpec((TM, H_RET * RET_DV), row(ident)),
        pl.BlockSpec((TM, H_RET * RET_DV), row(ident)),
    )
    return pl.pallas_call(
        _even_in_kernel,
        grid=(n_tiles,),
        in_specs=[pl.BlockSpec((TM, D_MODEL), row(x_blk)),
                  pl.BlockSpec((1, 6, D_MODEL), lambda i: (mod_row(i), 0, 0)),
                  _full(gain.shape), _full(win.shape), _full(qan.shape), _full(wq.shape), _full(kvan.shape),
                  _full(wk.shape), _full(wv.shape), _full(qg.shape), _full(kg.shape), tab, tab, tab],
        out_specs=out_specs,
        out_shape=out_shape,
        compiler_params=_cparams(("arbitrary",)),
        name="even_in_proj",
    )(x, mod, gain, win, qan, wq, kvan, wk, wv, qg, kg, cos, sin_lo, sin_hi)


def _ctx_kv_kernel(ckv_ref, kr_ref, wk_ref, wv_ref, kg_ref, cos_ref, sl_ref, sh_ref, k_in, v_in, k_ref, v_ref):
    del k_in, v_in
    _kv_heads(ckv_ref[...].astype(BF16), kr_ref[...], wk_ref, wv_ref, kg_ref[...],
              cos_ref[...], sl_ref[...], sh_ref[...], k_ref, v_ref)


def _ctx_kv(cache_ckv, cache_kr_slab, wk, wv, kg, rope_ident, k_all, v_all):
    cos, sin_lo, sin_hi = rope_ident
    nb = cache_ckv.shape[0] // PAST_LEN
    blk = lambda b: (0, b * KV_BLOCKS_PER_SEQ, 0)
    any_spec = pl.BlockSpec(memory_space=pl.ANY)
    return pl.pallas_call(
        _ctx_kv_kernel,
        grid=(nb,),
        in_specs=[pl.BlockSpec((PAST_LEN, KV_RANK), lambda b: (b, 0)),
                  pl.BlockSpec((PAST_LEN, LANES), lambda b: (b, 0)),
                  _full(wk.shape), _full(wv.shape), _full(kg.shape),
                  _full(cos.shape), _full(cos.shape), _full(cos.shape), any_spec, any_spec],
        out_specs=(pl.BlockSpec((H_MLA, PAST_LEN, LANES), blk), pl.BlockSpec((N_PAIR, PAST_LEN, LANES), blk)),
        out_shape=(jax.ShapeDtypeStruct(k_all.shape, k_all.dtype), jax.ShapeDtypeStruct(v_all.shape, v_all.dtype)),
        input_output_aliases={8: 0, 9: 1},
        compiler_params=_cparams(("arbitrary",)),
        name="ctx_kv_heads",
    )(cache_ckv, cache_kr_slab, wk, wv, kg, cos, sin_lo, sin_hi, k_all, v_all)


def _attn_kernel(q_ref, k_ref, v_ref, *rest):
    o_ref = rest[-1]
    for pr in range(v_ref.shape[0]):
        v = v_ref[pr]
        outs = []
        for a in range(2):
            h = 2 * pr + a
            s = lax.dot_general(q_ref[h], k_ref[h], (((1,), (1,)), ((), ())), preferred_element_type=F32)
            p = jnp.exp(s - jnp.max(s, axis=-1, keepdims=True))
            l = jnp.sum(p, axis=-1, keepdims=True)
            outs.append(jnp.dot(p.astype(BF16), v, preferred_element_type=F32) / l)
        lane = lax.broadcasted_iota(jnp.int32, outs[0].shape, 1)
        o_ref[pr] = jnp.where(lane < MLA_V, outs[0], outs[1]).astype(BF16)


def _attention(q, k, v, o_prev, *, nb, tq, lk, o_row0, pairs):
    nq = q.shape[1] // (nb * tq)
    o_blk0 = o_row0 // tq
    in_specs = [pl.BlockSpec((2 * pairs, tq, LANES), lambda b, p, i: (p, b * nq + i, 0)),
                pl.BlockSpec((2 * pairs, lk, LANES), lambda b, p, i: (p, b, 0)),
                pl.BlockSpec((pairs, lk, LANES), lambda b, p, i: (p, b, 0))]
    args = [q, k, v]
    aliases = {}
    if o_prev is not None:
        in_specs.append(pl.BlockSpec(memory_space=pl.ANY))
        args.append(o_prev)
        aliases = {3: 0}
    return pl.pallas_call(
        _attn_kernel,
        grid=(nb, N_PAIR // pairs, nq),
        in_specs=in_specs,
        out_specs=pl.BlockSpec((pairs, tq, LANES), lambda b, p, i: (p, o_blk0 + b * nq + i, 0)),
        out_shape=jax.ShapeDtypeStruct((N_PAIR, NT, LANES), BF16),
        input_output_aliases=aliases,
        compiler_params=_cparams(("arbitrary", "arbitrary", "arbitrary")),
        name="mla_attention",
    )(*args)


def _ret_kernel(dec_ref, q_ref, k_ref, v_ref, s0_ref, *rest, reverse, finish):
    if finish:
        of_ref, g_ref, gn_ref = rest[:3]
    o_ref, st_ref, s_scr = rest[-3:]
    j = pl.program_id(1)
    d = 1 if reverse else 0
    C = RET_CHUNK

    @pl.when(j == 0)
    def _():
        s_scr[...] = s0_ref[0]

    row = lax.broadcasted_iota(jnp.int32, (C, C), 0).astype(F32)
    col = lax.broadcasted_iota(jnp.int32, (C, C), 1).astype(F32)
    dist = (col - row) if reverse else (row - col)
    live = dist >= 0.0
    rowv = lax.broadcasted_iota(jnp.int32, (C, RET_DV), 0).astype(F32)
    rowk = lax.broadcasted_iota(jnp.int32, (C, RET_DK), 0).astype(F32)
    for h in range(H_RET):
        dl = jnp.full((1, LANES), dec_ref[d, h], F32)
        lg = jnp.minimum(dl, 0.0) - jnp.log1p(jnp.exp(-jnp.abs(dl)))
        q = q_ref[:, h * RET_DK:(h + 1) * RET_DK]
        k = k_ref[:, h * RET_DK:(h + 1) * RET_DK]
        v = v_ref[:, h * RET_DV:(h + 1) * RET_DV]
        dmask = jnp.where(live, jnp.exp(jnp.where(live, dist, 0.0) * lg), 0.0)
        scores = lax.dot_general(q, k, (((1,), (1,)), ((), ())), preferred_element_type=F32) * dmask
        inner = jnp.dot(scores.astype(BF16), v, preferred_element_type=F32)
        state = s_scr[h]
        xi = jnp.exp(((C - rowv) if reverse else (rowv + 1.0)) * lg)
        cross = jnp.dot(q, state.astype(BF16), preferred_element_type=F32) * xi
        wk = jnp.exp((rowk if reverse else (C - 1.0 - rowk)) * lg[:, :RET_DK])
        kw = (k.astype(F32) * wk).astype(BF16)
        u = lax.dot_general(kw, v, (((0,), (0,)), ((), ())), preferred_element_type=F32)
        new_state = jnp.exp(C * lg) * state + u
        s_scr[h] = new_state
        st_ref[0, h] = new_state
        o = inner + cross
        if finish:
            o = o + of_ref[:, h * RET_DV:(h + 1) * RET_DV]
            mu = jnp.mean(o, axis=-1, keepdims=True)
            var = jnp.mean(jnp.square(o - mu), axis=-1, keepdims=True)
            o = (o - mu) * lax.rsqrt(var + EPS) * gn_ref[:, h * RET_DV:(h + 1) * RET_DV]
            o = _silu(g_ref[:, h * RET_DV:(h + 1) * RET_DV]) * o
        o_ref[:, h * RET_DV:(h + 1) * RET_DV] = o.astype(o_ref.dtype)


def _retention_pass(decay, rq, rk, rv, s0, *, nb, nc, reverse, o_fwd=None, gate=None, gn=None, o_prev=None, o_blk0=0):
    C = RET_CHUNK
    chunk = (lambda b, j: (b * nc + (nc - 1 - j), 0)) if reverse else (lambda b, j: (b * nc + j, 0))
    in_specs = [pl.BlockSpec(memory_space=pltpu.SMEM),
                pl.BlockSpec((C, H_RET * RET_DK), chunk),
                pl.BlockSpec((C, H_RET * RET_DK), chunk),
                pl.BlockSpec((C, H_RET * RET_DV), chunk),
                pl.BlockSpec((1, H_RET, RET_DK, RET_DV), lambda b, j: (b, 0, 0, 0))]
    args = [decay, rq, rk, rv, s0]
    finish = o_fwd is not None
    aliases = {}
    if finish:
        in_specs += [pl.BlockSpec((C, H_RET * RET_DV), chunk), pl.BlockSpec((C, H_RET * RET_DV), chunk),
                     _full(gn.shape)]
        args += [o_fwd, gate, gn]
        if o_prev is not None:
            in_specs.append(pl.BlockSpec(memory_space=pl.ANY))
            args.append(o_prev)
            aliases = {len(args) - 1: 0}
        o_spec = pl.BlockSpec((C, H_RET * RET_DV), lambda b, j: (o_blk0 + b * nc + (nc - 1 - j), 0))
        o_shape = jax.ShapeDtypeStruct((NT, H_RET * RET_DV), BF16)
    else:
        o_spec = pl.BlockSpec((C, H_RET * RET_DV), chunk)
        o_shape = jax.ShapeDtypeStruct((nb * nc * C, H_RET * RET_DV), F32)
    return pl.pallas_call(
        functools.partial(_ret_kernel, reverse=reverse, finish=finish),
        grid=(nb, nc),
        in_specs=in_specs,
        out_specs=(o_spec, pl.BlockSpec((1, H_RET, RET_DK, RET_DV), lambda b, j: (b, 0, 0, 0))),
        out_shape=(o_shape, jax.ShapeDtypeStruct((nb, H_RET, RET_DK, RET_DV), F32)),
        scratch_shapes=[pltpu.VMEM((H_RET, RET_DK, RET_DV), F32)],
        input_output_aliases=aliases,
        compiler_params=_cparams(("arbitrary", "arbitrary")),
        name="retention_bwd" if reverse else "retention_fwd",
    )(*args)


def _row_copy(src_ref, s, dst_ref, d, sem):
    return pltpu.make_async_copy(src_ref.at[pl.ds(s, 1), :], dst_ref.at[pl.ds(d, 1), :], sem)


def _wait_rows(src_ref, dst_ref, sem, n):
    def body(r, c):
        _row_copy(src_ref, 0, dst_ref, 0, sem).wait()
        return c
    lax.fori_loop(0, n, body, 0, unroll=ISSUE_UNROLL)


def _residual_and_route(x, y, mod_ref, gffn_ref, wrh_ref, wrl_ref, br_ref, x1_ref, route_ref, cnt_ref, xbuf_ref,
                        cnt_scr, stage_scr, slot_vm, slot_sm, row_sem, slot_sem):
    i = pl.program_id(0)
    n_steps = pl.num_programs(0)

    @pl.when(i == 0)
    def _():
        cnt_scr[...] = jnp.zeros_like(cnt_scr)

    x1 = x + mod_ref[0, 2:3, :] * y
    x1_ref[...] = x1
    h2 = _prenorm(x1, gffn_ref[...], mod_ref[0, 3:4, :], mod_ref[0, 4:5, :])
    hi = h2.astype(BF16)
    hi32 = hi.astype(F32)
    lo = (h2 - hi32).astype(BF16)
    wrh = wrh_ref[...]
    lg = (jnp.dot(hi, wrh, preferred_element_type=F32) + jnp.dot(lo, wrh, preferred_element_type=F32)
          + jnp.dot(hi, wrl_ref[...], preferred_element_type=F32) + br_ref[...])
    lane = lax.broadcasted_iota(jnp.int32, lg.shape, 1).astype(F32)
    big = float(4 * LANES)
    gl = jnp.where((lane >= N_EXPERTS) & (lane < N_EXPERTS + N_GROUPS), lg, NEG)
    gmax = jnp.max(gl, axis=-1, keepdims=True)
    g_w = 1.0 / jnp.sum(jnp.exp(gl - gmax), axis=-1, keepdims=True)
    g_lane = jnp.min(jnp.where(gl == gmax, lane, big), axis=-1, keepdims=True)
    e_lo = (g_lane - N_EXPERTS) * EXP_PER_GROUP
    el = jnp.where((lane >= e_lo) & (lane < e_lo + EXP_PER_GROUP), lg, NEG)
    m1 = jnp.max(el, axis=-1, keepdims=True)
    i1 = jnp.min(jnp.where(el == m1, lane, big), axis=-1, keepdims=True)
    el2 = jnp.where(lane == i1, NEG, el)
    m2 = jnp.max(el2, axis=-1, keepdims=True)
    i2 = jnp.min(jnp.where(el2 == m2, lane, big), axis=-1, keepdims=True)
    t = jnp.exp(m2 - m1)
    w1 = g_w / (1.0 + t)
    w2 = w1 * t
    oh1 = lane == i1
    oh2 = lane == i2
    oh = jnp.where(oh1 | oh2, 1.0, 0.0)
    rr = lax.broadcasted_iota(jnp.int32, (TM, TM), 0)
    cc = lax.broadcasted_iota(jnp.int32, (TM, TM), 1)
    strict_lower = jnp.where(cc < rr, 1.0, 0.0).astype(BF16)
    before = jnp.dot(strict_lower, oh.astype(BF16), preferred_element_type=F32) + cnt_scr[...]
    slot1 = jnp.sum(jnp.where(oh1, before, 0.0), axis=-1, keepdims=True) + i1 * CAP
    slot2 = jnp.sum(jnp.where(oh2, before, 0.0), axis=-1, keepdims=True) + i2 * CAP
    cnt = cnt_scr[...] + jnp.sum(oh, axis=0, keepdims=True)
    cnt_scr[...] = cnt
    cnt_ref[...] = cnt
    route = jnp.where(lane == 0.0, i1, jnp.where(lane == 1.0, i2, jnp.where(lane == 2.0, w1, jnp.where(
        lane == 3.0, w2, jnp.where(lane == 4.0, slot1, jnp.where(lane == 5.0, slot2, 0.0))))))
    route_ref[...] = route

    @pl.when(i > 0)
    def _():
        _wait_rows(stage_scr.at[0], xbuf_ref, row_sem, 2 * TM)

    word = (lax.bitcast_convert_type(hi32[:, :PACKED], U32)
            | (lax.bitcast_convert_type(hi32[:, PACKED:], U32) >> 16))
    stage_scr[...] = word.reshape(stage_scr.shape)
    slot_vm[...] = jnp.transpose(route)[0:SUBLANES, :].astype(jnp.int32)
    to_smem = pltpu.make_async_copy(slot_vm, slot_sm, slot_sem)
    to_smem.start()
    to_smem.wait()

    def issue(g, c):
        for u in range(SUBLANES):
            src = stage_scr.at[g, pl.ds(u, 1), :]
            r = g * SUBLANES + u
            pltpu.make_async_copy(src, xbuf_ref.at[pl.ds(slot_sm[4, r], 1), :], row_sem).start()
            pltpu.make_async_copy(src, xbuf_ref.at[pl.ds(slot_sm[5, r], 1), :], row_sem).start()
        return c
    lax.fori_loop(0, TM // SUBLANES, issue, 0)

    @pl.when(i == n_steps - 1)
    def _():
        _wait_rows(stage_scr.at[0], xbuf_ref, row_sem, 2 * TM)


def _tail_specs():
    row = lambda i: (i, 0)
    in_specs = [pl.BlockSpec((1, 6, D_MODEL), lambda i: (_cond_row(i), 0, 0)),
                _full((1, D_MODEL)), _full((D_MODEL, LANES)), _full((D_MODEL, LANES)), _full((1, LANES))]
    out_specs = (pl.BlockSpec((TM, D_MODEL), row), pl.BlockSpec((TM, LANES), row), _full((1, LANES)),
                 pl.BlockSpec(memory_space=pl.ANY))
    out_shape = (jax.ShapeDtypeStruct((NT, D_MODEL), F32), jax.ShapeDtypeStruct((NT, LANES), F32),
                 jax.ShapeDtypeStruct((1, LANES), F32), jax.ShapeDtypeStruct((N_EXPERTS * CAP, PACKED), U32))
    scratch = [pltpu.VMEM((1, LANES), F32), pltpu.VMEM((TM // SUBLANES, SUBLANES, PACKED), U32),
               pltpu.VMEM((SUBLANES, TM), jnp.int32), pltpu.SMEM((SUBLANES, TM), jnp.int32),
               pltpu.SemaphoreType.DMA(()), pltpu.SemaphoreType.DMA(())]
    return in_specs, out_specs, out_shape, scratch


def _cond_row(i):
    return jnp.where(i < NP_TILES, DEC_BATCH, (i - NP_TILES) // S_TILES_PER_SEQ)


def _ctx_blk(i):
    return jnp.minimum(i, NP_TILES - 1)


def _lat_blk(i):
    return jnp.maximum(i - NP_TILES, 0)


def _even_out_kernel(xp_ref, xs_ref, o_ref, r_ref, wout_ref, *tail):
    y = jnp.dot(r_ref[...], wout_ref[H_MLA * MLA_V:, :], preferred_element_type=F32)
    for p in range(N_PAIR):
        y = y + jnp.dot(o_ref[p], wout_ref[p * LANES:(p + 1) * LANES, :], preferred_element_type=F32)
    x = jnp.where(pl.program_id(0) < NP_TILES, xp_ref[...], xs_ref[...])
    _residual_and_route(x, y, *tail)


def _even_out(xp, xs, o_mla, o_ret, wout, mod, gffn, wrh, wrl, br):
    row = lambda i: (i, 0)
    tail_in, out_specs, out_shape, scratch = _tail_specs()
    return pl.pallas_call(
        _even_out_kernel,
        grid=(NT_TILES,),
        in_specs=[pl.BlockSpec((TM, D_MODEL), lambda i: (_ctx_blk(i), 0)),
                  pl.BlockSpec((TM, D_MODEL), lambda i: (_lat_blk(i), 0)),
                  pl.BlockSpec((N_PAIR, TM, LANES), lambda i: (0, i, 0)),
                  pl.BlockSpec((TM, H_RET * RET_DV), row),
                  _full(wout.shape)] + tail_in,
        out_specs=out_specs,
        out_shape=out_shape,
        scratch_shapes=scratch,
        compiler_params=_cparams(("arbitrary",)),
        name="even_out_route",
    )(xp, xs, o_mla, o_ret, wout, mod, gffn, wrh, wrl, br)


def _conf_in_kernel(x_ref, mod_ref, g_ref, win_ref, u_ref):
    h = _prenorm(x_ref[...], g_ref[...], mod_ref[0, 0:1, :], mod_ref[0, 1:2, :])
    z = jnp.dot(h.astype(BF16), win_ref[...], preferred_element_type=F32)
    u_ref[...] = z[:, :CONV_DIM] * jax.nn.sigmoid(z[:, CONV_DIM:])


def _conf_in(x, mod, gain, win):
    row = lambda i: (i, 0)
    return pl.pallas_call(
        _conf_in_kernel,
        grid=(NT_TILES,),
        in_specs=[pl.BlockSpec((TM, D_MODEL), row),
                  pl.BlockSpec((1, 6, D_MODEL), lambda i: (_cond_row(i), 0, 0)),
                  _full(gain.shape), _full(win.shape)],
        out_specs=pl.BlockSpec((TM, CONV_DIM), row),
        out_shape=jax.ShapeDtypeStruct((NT, CONV_DIM), F32),
        compiler_params=_cparams(("arbitrary",)),
        name="conformer_in_glu",
    )(x, mod, gain, win)


CONV_ROWS = 32
SHIFT_ROWS = TM + 2 * HALO - SUBLANES


def _conv_out_kernel(x_ref, u_ref, ul_ref, ur_ref, dw_ref, dwb_ref, lng_ref, lnb_ref, wout_ref, *tail_and_scratch):
    tail = tail_and_scratch[:-3]
    ext_scr, shift_scr, act_scr = tail_and_scratch[-3:]
    i = pl.program_id(0)
    t = (i - NP_TILES) % S_TILES_PER_SEQ
    first = (i < NP_TILES) | (t == 0)
    last = (i < NP_TILES) | (t == S_TILES_PER_SEQ - 1)
    ext_scr[0:HALO, :] = jnp.where(first, 0.0, ul_ref[...])
    ext_scr[HALO:HALO + TM, :] = u_ref[...]
    ext_scr[HALO + TM:, :] = jnp.where(last, 0.0, ur_ref[...])
    for s in range(SUBLANES):
        shift_scr[s] = ext_scr[pl.ds(s, SHIFT_ROWS), :]
    off = HALO - CONV_W // 2
    for rb in range(TM // CONV_ROWS):
        acc = jnp.zeros((CONV_ROWS, CONV_DIM), F32)
        for kk in range(CONV_W):
            s, a = (off + kk) % SUBLANES, (off + kk) // SUBLANES * SUBLANES
            acc = acc + shift_scr[s, pl.ds(rb * CONV_ROWS + a, CONV_ROWS), :] * dw_ref[kk:kk + 1, :]
        c = acc + dwb_ref[...]
        mu = jnp.mean(c, axis=-1, keepdims=True)
        var = jnp.mean(jnp.square(c - mu), axis=-1, keepdims=True)
        c = (c - mu) * lax.rsqrt(var + EPS) * lng_ref[...] + lnb_ref[...]
        act_scr[rb * CONV_ROWS:(rb + 1) * CONV_ROWS, :] = _silu(c).astype(BF16)
    y = jnp.dot(act_scr[...], wout_ref[...], preferred_element_type=F32)
    _residual_and_route(x_ref[...], y, *tail)


def _conv_out(x, u, dw, dwb, lng, lnb, wout, mod, gffn, wrh, wrl, br):
    row = lambda i: (i, 0)
    per = TM // HALO
    n_halo = NT // HALO
    tail_in, out_specs, out_shape, scratch = _tail_specs()
    return pl.pallas_call(
        _conv_out_kernel,
        grid=(NT_TILES,),
        in_specs=[pl.BlockSpec((TM, D_MODEL), row),
                  pl.BlockSpec((TM, CONV_DIM), row),
                  pl.BlockSpec((HALO, CONV_DIM), lambda i: (jnp.maximum(i * per - 1, 0), 0)),
                  pl.BlockSpec((HALO, CONV_DIM), lambda i: (jnp.minimum((i + 1) * per, n_halo - 1), 0)),
                  _full(dw.shape), _full(dwb.shape), _full(lng.shape), _full(lnb.shape), _full(wout.shape)] + tail_in,
        out_specs=out_specs,
        out_shape=out_shape,
        scratch_shapes=scratch + [pltpu.VMEM((TM + 2 * HALO, CONV_DIM), F32),
                                  pltpu.VMEM((SUBLANES, SHIFT_ROWS, CONV_DIM), F32),
                                  pltpu.VMEM((TM, CONV_DIM), BF16)],
        compiler_params=_cparams(("arbitrary",)),
        name="conformer_conv_out_route",
    )(x, u, u, u, dw, dwb, lng, lnb, wout, mod, gffn, wrh, wrl, br)


def _pad_fill_kernel(cnt_ref, xin_ref, xbuf_ref, zero_scr, sem):
    del xin_ref
    e = pl.program_id(0)
    n = cnt_ref[e]
    pad = (EB - n % EB) % EB
    zero_scr[...] = jnp.zeros_like(zero_scr)

    def start(r, c):
        _row_copy(zero_scr, 0, xbuf_ref, e * CAP + n + r, sem).start()
        return c

    def wait(r, c):
        _row_copy(zero_scr, 0, xbuf_ref, e * CAP + n + r, sem).wait()
        return c

    lax.fori_loop(0, pad, start, 0)
    lax.fori_loop(0, pad, wait, 0)


def _pad_fill(counts, xbuf):
    return pl.pallas_call(
        _pad_fill_kernel,
        grid_spec=pltpu.PrefetchScalarGridSpec(
            num_scalar_prefetch=1,
            grid=(N_EXPERTS,),
            in_specs=[pl.BlockSpec(memory_space=pl.ANY)],
            out_specs=pl.BlockSpec(memory_space=pl.ANY),
            scratch_shapes=[pltpu.VMEM((SUBLANES, PACKED), U32), pltpu.SemaphoreType.DMA(())]),
        out_shape=jax.ShapeDtypeStruct(xbuf.shape, xbuf.dtype),
        input_output_aliases={1: 0},
        compiler_params=_cparams(("arbitrary",)),
        name="moe_pad_fill",
    )(counts, xbuf)


def _expert_kernel(blk_ref, exp_ref, flag_ref, x_ref, wu_ref, wd_ref, o_ref, wu_scr, wd_scr):
    i = pl.program_id(0)
    flag = flag_ref[i]

    @pl.when((flag & 2) != 0)
    def _():
        wu_scr[...] = wu_ref[0].astype(BF16)
        wd_scr[...] = wd_ref[0].astype(BF16)

    @pl.when((flag & 1) != 0)
    def _():
        w = x_ref[...]
        xa = lax.bitcast_convert_type(w & jnp.uint32(0xFFFF0000), F32).astype(BF16)
        xb = lax.bitcast_convert_type(w << 16, F32).astype(BF16)
        ab = (jnp.dot(xa, wu_scr[:PACKED, :], preferred_element_type=F32)
              + jnp.dot(xb, wu_scr[PACKED:, :], preferred_element_type=F32))
        mid = (_silu(ab[:, :D_EXPERT]) * ab[:, D_EXPERT:]).astype(BF16)
        o_ref[...] = jnp.dot(mid, wd_scr[...], preferred_element_type=F32)


def _experts(blk, exp, flag, xbuf, w_up, w_down):
    return pl.pallas_call(
        _expert_kernel,
        grid_spec=pltpu.PrefetchScalarGridSpec(
            num_scalar_prefetch=3,
            grid=(N_ITEMS,),
            in_specs=[pl.BlockSpec((EB, PACKED), lambda i, b, e, f: (b[i], 0)),
                      pl.BlockSpec((1, D_MODEL, 2 * D_EXPERT), lambda i, b, e, f: (e[i], 0, 0)),
                      pl.BlockSpec((1, D_EXPERT, D_MODEL), lambda i, b, e, f: (e[i], 0, 0))],
            out_specs=pl.BlockSpec((EB, D_MODEL), lambda i, b, e, f: (b[i], 0)),
            scratch_shapes=[pltpu.VMEM((D_MODEL, 2 * D_EXPERT), BF16), pltpu.VMEM((D_EXPERT, D_MODEL), BF16)]),
        out_shape=jax.ShapeDtypeStruct((N_EXPERTS * CAP, D_MODEL), F32),
        compiler_params=_cparams(("arbitrary",)),
        name="moe_experts",
    )(blk, exp, flag, xbuf, w_up, w_down)


def _combine_kernel(slot_ref, x_ref, route_ref, mod_ref, ybuf_ref, *rest):
    outs, (g_scr, sem) = rest[:-2], rest[-2:]
    i = pl.program_id(0)
    n_steps = pl.num_programs(0)

    def gather(tile, buf):
        def issue(r, c):
            base = tile * (2 * TM) + 2 * r
            _row_copy(ybuf_ref, slot_ref[base], g_scr.at[buf], r, sem.at[buf]).start()
            _row_copy(ybuf_ref, slot_ref[base + 1], g_scr.at[buf], TM + r, sem.at[buf]).start()
            return c
        lax.fori_loop(0, TM, issue, 0, unroll=ISSUE_UNROLL)

    cur = i % 2

    @pl.when(i == 0)
    def _():
        gather(0, 0)

    @pl.when(i + 1 < n_steps)
    def _():
        gather(i + 1, 1 - cur)

    _wait_rows(ybuf_ref, g_scr.at[cur], sem.at[cur], 2 * TM)
    route = route_ref[...]
    y = route[:, 2:3] * g_scr[cur, 0:TM, :] + route[:, 3:4] * g_scr[cur, TM:2 * TM, :]
    res = x_ref[...] + mod_ref[0, 5:6, :] * y
    if len(outs) == 1:
        outs[0][...] = res
    else:
        @pl.when(i < NP_TILES)
        def _():
            outs[0][...] = res

        @pl.when(i >= NP_TILES)
        def _():
            outs[1][...] = res


def _combine(slots, x1, route, mod, ybuf, *, split):
    row = lambda i, s: (i, 0)
    if split:
        out_specs = (pl.BlockSpec((TM, D_MODEL), lambda i, s: (_ctx_blk(i), 0)),
                     pl.BlockSpec((TM, D_MODEL), lambda i, s: (_lat_blk(i), 0)))
        out_shape = (jax.ShapeDtypeStruct((NP, D_MODEL), F32), jax.ShapeDtypeStruct((NS, D_MODEL), F32))
    else:
        out_specs = pl.BlockSpec((TM, D_MODEL), row)
        out_shape = jax.ShapeDtypeStruct((NT, D_MODEL), F32)
    return pl.pallas_call(
        _combine_kernel,
        grid_spec=pltpu.PrefetchScalarGridSpec(
            num_scalar_prefetch=1,
            grid=(NT_TILES,),
            in_specs=[pl.BlockSpec((TM, D_MODEL), row),
                      pl.BlockSpec((TM, LANES), row),
                      pl.BlockSpec((1, 6, D_MODEL), lambda i, s: (_cond_row(i), 0, 0)),
                      pl.BlockSpec(memory_space=pl.ANY)],
            out_specs=out_specs,
            scratch_shapes=[pltpu.VMEM((2, 2 * TM, D_MODEL), F32), pltpu.SemaphoreType.DMA((2,))]),
        out_shape=out_shape,
        compiler_params=_cparams(("arbitrary",)),
        name="moe_combine",
    )(slots, x1, route, mod, ybuf)


def _moe(x1, xbuf, route, counts, mod, w_up, w_down, *, split):
    slots = route[:, 4:6].astype(jnp.int32).reshape(-1)
    cnt = counts[0, :N_EXPERTS].astype(jnp.int32)
    nblk = (cnt + EB - 1) // EB
    ends = jnp.cumsum(nblk)
    total = ends[-1]
    item = jnp.arange(N_ITEMS, dtype=jnp.int32)
    valid = item < total
    item_c = jnp.minimum(item, total - 1)
    exp = jnp.minimum(jnp.sum((item_c[:, None] >= ends[None, :]).astype(jnp.int32), axis=1), N_EXPERTS - 1)
    j = item_c - (ends[exp] - nblk[exp])
    blk = (exp * CAP_BLOCKS + j).astype(jnp.int32)
    flag = (valid.astype(jnp.int32) + 2 * (valid & (j == 0)).astype(jnp.int32)).astype(jnp.int32)

    xbuf = _pad_fill(cnt, xbuf)
    ybuf = _experts(blk, exp, flag, xbuf, w_up, w_down)
    return _combine(slots, x1, route, mod, ybuf, split=split)


def _pad_heads(w, width, real):
    lead = w.shape[:-1]
    w = w.reshape(lead + (H_MLA, real))
    w = jnp.pad(w, [(0, 0)] * len(lead) + [(0, 0), (0, width - real)])
    return w.reshape(lead + (H_MLA * width,))


def _rope_tables():
    L = DEC_SEQ
    rows = L // GRID_W
    r = jnp.repeat(jnp.arange(rows, dtype=F32), GRID_W)
    col = jnp.tile(jnp.arange(GRID_W, dtype=F32), rows)
    n_f = ROPE_DIM // 4
    freqs = ROPE_BASE ** (-jnp.arange(n_f, dtype=F32) / n_f)
    ang = jnp.concatenate([r[:, None] * freqs, col[:, None] * freqs], axis=-1)
    cos, sin = jnp.cos(ang), jnp.sin(ang)
    half = ROPE_DIM // 2
    z = lambda n: jnp.zeros((L, n), F32)
    o = lambda n: jnp.ones((L, n), F32)
    cos_t = jnp.concatenate([o(MLA_NOPE), cos, cos, o(LANES - MLA_QK)], axis=-1)
    sin_lo = jnp.concatenate([z(MLA_NOPE), -sin, z(half), z(LANES - MLA_QK)], axis=-1)
    sin_hi = jnp.concatenate([z(MLA_NOPE), z(half), sin, z(LANES - MLA_QK)], axis=-1)
    ident = (jnp.ones((TM, LANES), F32), jnp.zeros((TM, LANES), F32), jnp.zeros((TM, LANES), F32))
    return (cos_t, sin_lo, sin_hi), ident


def _even_weights(w_in, q_a_norm, w_q_up, kv_a_norm, w_kv_up, q_norm, k_norm):
    d = D_MODEL
    kr_cols = jnp.zeros((d, LANES), F32).at[:, MLA_NOPE:MLA_QK].set(w_in[:, Q_RANK + KV_RANK:Q_RANK + KV_RANK + ROPE_DIM])
    rest = w_in[:, Q_RANK + KV_RANK + ROPE_DIM:]
    win = jnp.concatenate([w_in[:, :Q_RANK + KV_RANK], kr_cols, rest], axis=-1).astype(BF16)
    wq = _pad_heads(w_q_up, LANES, MLA_QK).astype(BF16)
    kv = w_kv_up.reshape(KV_RANK, H_MLA, MLA_NOPE + MLA_V)
    wk = _pad_heads(kv[:, :, :MLA_NOPE].reshape(KV_RANK, H_MLA * MLA_NOPE), LANES, MLA_NOPE).astype(BF16)
    wv = kv[:, :, MLA_NOPE:].reshape(KV_RANK, H_MLA * MLA_V).astype(BF16)
    padg = lambda g: jnp.pad(g, (0, LANES - MLA_QK)).reshape(1, LANES)
    qg = padg(q_norm) * (MLA_QK ** -0.5)
    kg = padg(k_norm)
    return (win, q_a_norm.reshape(1, -1), wq, kv_a_norm.reshape(1, -1), wk, wv, qg, kg)


def _router_weights(w_group, b_group, w_expert, b_expert):
    w = jnp.zeros((D_MODEL, LANES), F32).at[:, :N_EXPERTS].set(w_expert).at[:, N_EXPERTS:N_EXPERTS + N_GROUPS].set(w_group)
    b = jnp.zeros((1, LANES), F32).at[0, :N_EXPERTS].set(b_expert).at[0, N_EXPERTS:N_EXPERTS + N_GROUPS].set(b_group)
    hi = w.astype(BF16)
    lo = (w - hi.astype(F32)).astype(BF16)
    return hi, lo, b


def kernel(x_prompt, x_sample, cache_mla_ckv, cache_mla_krope, state_retention, c, c_ctx, ada_w, ada_b, norm_mix, norm_ffn, ev_w_in, ev_q_a_norm, ev_w_q_up, ev_kv_a_norm, ev_w_kv_up, ev_q_norm, ev_k_norm, ev_ret_decay, ev_ret_gn, ev_w_out, od_w_in, od_dw, od_dw_b, od_ln_g, od_ln_b, od_w_out, moe_w_group, moe_b_group, moe_w_expert, moe_b_expert, moe_w_up, moe_w_down):
    depth = ada_w.shape[0]
    x, xp, xs = None, x_prompt.reshape(NP, D_MODEL), x_sample.reshape(NS, D_MODEL)
    cond = jnp.concatenate([c, c_ctx[None, :], jnp.zeros((2 * SUBLANES - DEC_BATCH - 1, D_MODEL), F32)], axis=0)
    mods = _ada(cond, ada_w, ada_b).reshape(depth, cond.shape[0], 6, D_MODEL)
    rope, rope_ident = _rope_tables()
    new_ckv, new_krope, new_state = [], [], []

    for l in range(depth):
        mod = mods[l]
        jj = l // 2
        router = _router_weights(moe_w_group[l], moe_b_group[l], moe_w_expert[l], moe_b_expert[l])
        gmix = norm_mix[l].reshape(1, D_MODEL)
        gffn = norm_ffn[l].reshape(1, D_MODEL)
        if l % 2 == 0:
            wts = _even_weights(ev_w_in[jj], ev_q_a_norm[jj], ev_w_q_up[jj], ev_kv_a_norm[jj], ev_w_kv_up[jj],
                                ev_q_norm[jj], ev_k_norm[jj])
            if xp is None:
                xp, xs = x[:NP], x[NP:]
            qp, kp, vp, ckv_p, kr_p, rq_p, rk_p, rv_p, rg_p = _even_in(
                xp, mod, gmix, wts, rope_ident, n_tiles=NP_TILES, x_blk=lambda i: i, mod_row=lambda i: DEC_BATCH,
                rope_blk=lambda i: 0, kv_rows=NP, kv_blk=lambda i: i)
            qs, ks, vs, _, _, rq_s, rk_s, rv_s, rg_s = _even_in(
                xs, mod, gmix, wts, rope, n_tiles=NS // TM, x_blk=lambda i: i,
                mod_row=lambda i: i // S_TILES_PER_SEQ, rope_blk=lambda i: i % S_TILES_PER_SEQ,
                kv_rows=DEC_BATCH * LK_S,
                kv_blk=lambda i: (i // S_TILES_PER_SEQ) * KV_BLOCKS_PER_SEQ + 1 + i % S_TILES_PER_SEQ)
            kr_cache = jnp.pad(cache_mla_krope[:, jj].reshape(DEC_BATCH * PAST_LEN, ROPE_DIM),
                               ((0, 0), (MLA_NOPE, LANES - MLA_QK)))
            ks, vs = _ctx_kv(cache_mla_ckv[:, jj].reshape(DEC_BATCH * PAST_LEN, KV_RANK), kr_cache,
                             wts[4], wts[5], wts[7], rope_ident, ks, vs)
            o_mla = _attention(qp, kp, vp, None, nb=BATCH, tq=SEQ, lk=SEQ, o_row0=0, pairs=N_PAIR)
            o_mla = _attention(qs, ks, vs, o_mla, nb=DEC_BATCH, tq=TQ_LATENT, lk=LK_S, o_row0=NP, pairs=N_PAIR)

            decay = ev_ret_decay[jj]
            gn = ev_ret_gn[jj].reshape(1, -1)
            zero_state = jnp.zeros((BATCH, H_RET, RET_DK, RET_DV), F32)
            ncp, ncs = SEQ // RET_CHUNK, DEC_SEQ // RET_CHUNK
            of_p, sf_p = _retention_pass(decay, rq_p, rk_p, rv_p, zero_state, nb=BATCH, nc=ncp, reverse=False)
            o_ret, sb_p = _retention_pass(decay, rq_p, rk_p, rv_p, zero_state, nb=BATCH, nc=ncp, reverse=True,
                                          o_fwd=of_p, gate=rg_p, gn=gn)
            of_s, _ = _retention_pass(decay, rq_s, rk_s, rv_s, state_retention[:, jj, 0], nb=DEC_BATCH, nc=ncs,
                                      reverse=False)
            o_ret, _ = _retention_pass(decay, rq_s, rk_s, rv_s, state_retention[:, jj, 1], nb=DEC_BATCH, nc=ncs,
                                       reverse=True, o_fwd=of_s, gate=rg_s, gn=gn, o_prev=o_ret,
                                       o_blk0=NP // RET_CHUNK)
            new_ckv.append(ckv_p.reshape(BATCH, SEQ, KV_RANK))
            new_krope.append(kr_p[:, MLA_NOPE:MLA_QK].reshape(BATCH, SEQ, ROPE_DIM))
            new_state.append(jnp.stack([sf_p, sb_p], axis=1))
            x1, route, counts, xbuf = _even_out(xp, xs, o_mla, o_ret, ev_w_out[jj].astype(BF16), mod, gffn, *router)
        else:
            x = jnp.concatenate([xp, xs], axis=0) if x is None else x
            u = _conf_in(x, mod, gmix, od_w_in[jj].astype(BF16))
            x1, route, counts, xbuf = _conv_out(
                x, u, od_dw[jj], od_dw_b[jj].reshape(1, -1), od_ln_g[jj].reshape(1, -1), od_ln_b[jj].reshape(1, -1),
                od_w_out[jj].astype(BF16), mod, gffn, *router)
        if l == depth - 1:
            xp, xs = _moe(x1, xbuf, route, counts, mod, moe_w_up[l], moe_w_down[l], split=True)
            x = None
        else:
            x = _moe(x1, xbuf, route, counts, mod, moe_w_up[l], moe_w_down[l], split=False)
            xp, xs = None, None

    return (xp.reshape(BATCH, SEQ, D_MODEL), xs.reshape(DEC_BATCH, DEC_SEQ, D_MODEL),
            jnp.stack(new_ckv, axis=1), jnp.stack(new_krope, axis=1), jnp.stack(new_state, axis=1))
```

```python
import functools

import jax
import jax.numpy as jnp
from jax import lax
from jax.experimental import pallas as pl
from jax.experimental.pallas import tpu as pltpu

F32 = jnp.float32
BF16 = jnp.bfloat16
U32 = jnp.uint32

D_MODEL = 1024
BATCH, SEQ = 32, 256
DEC_BATCH, DEC_SEQ = 8, 4096
PAST_LEN = 256
GRID_W = 64
EPS = 1e-6
H_MLA, Q_RANK, KV_RANK = 8, 256, 128
MLA_NOPE, ROPE_DIM, MLA_V = 64, 32, 64
MLA_QK = MLA_NOPE + ROPE_DIM
ROPE_BASE = 10000.0
H_RET, RET_DK, RET_DV, RET_CHUNK = 4, 64, 128, 128
CONV_DIM, CONV_W = 1024, 31
N_GROUPS, EXP_PER_GROUP, D_EXPERT = 4, 8, 256
N_EXPERTS = N_GROUPS * EXP_PER_GROUP

LANES = 128
SUBLANES = 8
VMEM_LIMIT = 48 * 1024 * 1024

NP = BATCH * SEQ
NS = DEC_BATCH * DEC_SEQ
NT = NP + NS
TM = 256
NT_TILES = NT // TM
NP_TILES = NP // TM
S_TILES_PER_SEQ = DEC_SEQ // TM
LK_S = PAST_LEN + DEC_SEQ
KV_BLOCKS_PER_SEQ = LK_S // TM
N_PAIR = H_MLA // 2
TQ_LATENT = 256
RET_STEP_ROWS = 2 * RET_CHUNK
EB = 256
CAP = NT + EB
CAP_BLOCKS = CAP // EB
N_ITEMS = (2 * NT) // EB + N_EXPERTS
HALO = 16
PACKED = D_MODEL // 2
ISSUE_UNROLL = 8
NEG = -1e30


def _cparams(sem):
    return pltpu.CompilerParams(dimension_semantics=sem, vmem_limit_bytes=VMEM_LIMIT)


def _full(shape):
    n = len(shape)
    return pl.BlockSpec(shape, lambda *_: (0,) * n)


def _rms(x, gain):
    return x * lax.rsqrt(jnp.mean(x * x, axis=-1, keepdims=True) + EPS) * gain


def _prenorm(x, gain, shift, scale):
    return _rms(x, gain) * (1.0 + scale) + shift


def _silu(x):
    return x * jax.nn.sigmoid(x)


def _rope128(x, cos, sin_lo, sin_hi):
    return x * cos + pltpu.roll(x, LANES - ROPE_DIM // 2, 1) * sin_lo + pltpu.roll(x, ROPE_DIM // 2, 1) * sin_hi


def _head_norm_rope(slab, gain, cos, sin_lo, sin_hi):
    r = lax.rsqrt(jnp.sum(slab * slab, axis=-1, keepdims=True) * (1.0 / MLA_QK) + EPS)
    return _rope128(slab * r * gain, cos, sin_lo, sin_hi)


def _ada_kernel(c_ref, w_ref, b_ref, o_ref):
    s = _silu(c_ref[...]).astype(BF16)
    o_ref[0] = jnp.dot(s, w_ref[0].astype(BF16), preferred_element_type=F32) + b_ref[0]


def _ada(cond, ada_w, ada_b):
    depth, d, n = ada_w.shape
    rows = cond.shape[0]
    tn = 1536
    return pl.pallas_call(
        _ada_kernel,
        grid=(depth, n // tn),
        in_specs=[pl.BlockSpec((rows, d), lambda l, j: (0, 0)),
                  pl.BlockSpec((1, d, tn), lambda l, j: (l, 0, j)),
                  pl.BlockSpec((1, 1, tn), lambda l, j: (l, 0, j))],
        out_specs=pl.BlockSpec((1, rows, tn), lambda l, j: (l, 0, j)),
        out_shape=jax.ShapeDtypeStruct((depth, rows, n), F32),
        compiler_params=_cparams(("arbitrary", "arbitrary")),
        name="ada_modulation",
    )(cond, ada_w, ada_b.reshape(depth, 1, n))


def _kv_heads(ckvn_bf, kr_slab, wk_ref, wv_ref, kg, cos, sin_lo, sin_hi, k_ref, v_ref):
    kk = jnp.dot(ckvn_bf, wk_ref[...], preferred_element_type=F32)
    vv = jnp.dot(ckvn_bf, wv_ref[...], preferred_element_type=F32)
    for h in range(H_MLA):
        kh = kk[:, h * LANES:(h + 1) * LANES] + kr_slab
        k_ref[h] = _head_norm_rope(kh, kg, cos, sin_lo, sin_hi).astype(BF16)
    for j in range(N_PAIR):
        v_ref[j] = vv[:, j * LANES:(j + 1) * LANES].astype(BF16)


def _even_in_kernel(x_ref, mod_ref, g_ref, win_ref, qan_ref, wq_ref, kvan_ref, wk_ref, wv_ref, qg_ref, kg_ref,
                    cos_ref, sl_ref, sh_ref,
                    q_ref, k_ref, v_ref, ckv_ref, kr_ref, rq_ref, rk_ref, rv_ref, rg_ref):
    h = _prenorm(x_ref[...], g_ref[...], mod_ref[0, 0:1, :], mod_ref[0, 1:2, :])
    z = jnp.dot(h.astype(BF16), win_ref[...], preferred_element_type=F32)
    cq = z[:, 0:256]
    ckv = z[:, 256:384]
    kr_slab = z[:, 384:512]
    cos, sin_lo, sin_hi = cos_ref[...], sl_ref[...], sh_ref[...]

    qq = jnp.dot(_rms(cq, qan_ref[...]).astype(BF16), wq_ref[...], preferred_element_type=F32)
    qg = qg_ref[...]
    for hh in range(H_MLA):
        q_ref[hh] = _head_norm_rope(qq[:, hh * LANES:(hh + 1) * LANES], qg, cos, sin_lo, sin_hi).astype(BF16)

    ckvn = _rms(ckv, kvan_ref[...])
    ckv_ref[...] = ckvn
    kr_ref[...] = kr_slab
    _kv_heads(ckvn.astype(BF16), kr_slab, wk_ref, wv_ref, kg_ref[...], cos, sin_lo, sin_hi, k_ref, v_ref)

    rq_ref[...] = z[:, 512:768].astype(BF16)
    rk_ref[...] = (z[:, 768:1024] * (RET_DK ** -0.5)).astype(BF16)
    rv_ref[...] = z[:, 1024:1536].astype(BF16)
    rg_ref[...] = z[:, 1536:2048]


def _even_in(x, mod, gain, wts, rope, *, n_tiles, x_blk, mod_row, rope_blk, kv_rows, kv_blk):
    win, qan, wq, kvan, wk, wv, qg, kg = wts
    cos, sin_lo, sin_hi = rope
    ntok = n_tiles * TM
    row = lambda f: (lambda i: (f(i), 0))
    tab = pl.BlockSpec((TM, LANES), row(rope_blk))
    heads = lambda f: (lambda i: (0, f(i), 0))
    out_shape = (
        jax.ShapeDtypeStruct((H_MLA, ntok, LANES), BF16),
        jax.ShapeDtypeStruct((H_MLA, kv_rows, LANES), BF16),
        jax.ShapeDtypeStruct((N_PAIR, kv_rows, LANES), BF16),
        jax.ShapeDtypeStruct((ntok, KV_RANK), F32),
        jax.ShapeDtypeStruct((ntok, LANES), F32),
        jax.ShapeDtypeStruct((ntok, H_RET * RET_DK), BF16),
        jax.ShapeDtypeStruct((ntok, H_RET * RET_DK), BF16),
        jax.ShapeDtypeStruct((ntok, H_RET * RET_DV), BF16),
        jax.ShapeDtypeStruct((ntok, H_RET * RET_DV), F32),
    )
    ident = lambda i: i
    out_specs = (
        pl.BlockSpec((H_MLA, TM, LANES), heads(ident)),
        pl.BlockSpec((H_MLA, TM, LANES), heads(kv_blk)),
        pl.BlockSpec((N_PAIR, TM, LANES), heads(kv_blk)),
        pl.BlockSpec((TM, KV_RANK), row(ident)),
        pl.BlockSpec((TM, LANES), row(ident)),
        pl.BlockSpec((TM, H_RET * RET_DK), row(ident)),
        pl.BlockSpec((TM, H_RET * RET_DK), row(ident)),
        pl.BlockSpec((TM, H_RET * RET_DV), row(ident)),
        pl.BlockSpec((TM, H_RET * RET_DV), row(ident)),
    )
    return pl.pallas_call(
        _even_in_kernel,
        grid=(n_tiles,),
        in_specs=[pl.BlockSpec((TM, D_MODEL), row(x_blk)),
                  pl.BlockSpec((1, 6, D_MODEL), lambda i: (mod_row(i), 0, 0)),
                  _full(gain.shape), _full(win.shape), _full(qan.shape), _full(wq.shape), _full(kvan.shape),
                  _full(wk.shape), _full(wv.shape), _full(qg.shape), _full(kg.shape), tab, tab, tab],
        out_specs=out_specs,
        out_shape=out_shape,
        compiler_params=_cparams(("arbitrary",)),
        name="even_in_proj",
    )(x, mod, gain, win, qan, wq, kvan, wk, wv, qg, kg, cos, sin_lo, sin_hi)


def _ctx_kv_kernel(ckv_ref, kr_ref, wk_ref, wv_ref, kg_ref, cos_ref, sl_ref, sh_ref, k_in, v_in, k_ref, v_ref):
    del k_in, v_in
    _kv_heads(ckv_ref[...].astype(BF16), kr_ref[...], wk_ref, wv_ref, kg_ref[...],
              cos_ref[...], sl_ref[...], sh_ref[...], k_ref, v_ref)


def _ctx_kv(cache_ckv, cache_kr_slab, wk, wv, kg, rope_ident, k_all, v_all):
    cos, sin_lo, sin_hi = rope_ident
    nb = cache_ckv.shape[0] // PAST_LEN
    blk = lambda b: (0, b * KV_BLOCKS_PER_SEQ, 0)
    any_spec = pl.BlockSpec(memory_space=pl.ANY)
    return pl.pallas_call(
        _ctx_kv_kernel,
        grid=(nb,),
        in_specs=[pl.BlockSpec((PAST_LEN, KV_RANK), lambda b: (b, 0)),
                  pl.BlockSpec((PAST_LEN, LANES), lambda b: (b, 0)),
                  _full(wk.shape), _full(wv.shape), _full(kg.shape),
                  _full(cos.shape), _full(cos.shape), _full(cos.shape), any_spec, any_spec],
        out_specs=(pl.BlockSpec((H_MLA, PAST_LEN, LANES), blk), pl.BlockSpec((N_PAIR, PAST_LEN, LANES), blk)),
        out_shape=(jax.ShapeDtypeStruct(k_all.shape, k_all.dtype), jax.ShapeDtypeStruct(v_all.shape, v_all.dtype)),
        input_output_aliases={8: 0, 9: 1},
        compiler_params=_cparams(("arbitrary",)),
        name="ctx_kv_heads",
    )(cache_ckv, cache_kr_slab, wk, wv, kg, cos, sin_lo, sin_hi, k_all, v_all)


def _attn_kernel(q_ref, k_ref, v_ref, *rest):
    o_ref = rest[-1]
    for pr in range(v_ref.shape[0]):
        v = v_ref[pr]
        outs = []
        for a in range(2):
            h = 2 * pr + a
            s = lax.dot_general(q_ref[h], k_ref[h], (((1,), (1,)), ((), ())), preferred_element_type=F32)
            p = jnp.exp(s - jnp.max(s, axis=-1, keepdims=True))
            l = jnp.sum(p, axis=-1, keepdims=True)
            outs.append(jnp.dot(p.astype(BF16), v, preferred_element_type=F32) / l)
        lane = lax.broadcasted_iota(jnp.int32, outs[0].shape, 1)
        o_ref[pr] = jnp.where(lane < MLA_V, outs[0], outs[1]).astype(BF16)


def _attention(q, k, v, o_prev, *, nb, tq, lk, o_row0, pairs):
    nq = q.shape[1] // (nb * tq)
    o_blk0 = o_row0 // tq
    in_specs = [pl.BlockSpec((2 * pairs, tq, LANES), lambda b, p, i: (p, b * nq + i, 0)),
                pl.BlockSpec((2 * pairs, lk, LANES), lambda b, p, i: (p, b, 0)),
                pl.BlockSpec((pairs, lk, LANES), lambda b, p, i: (p, b, 0))]
    args = [q, k, v]
    aliases = {}
    if o_prev is not None:
        in_specs.append(pl.BlockSpec(memory_space=pl.ANY))
        args.append(o_prev)
        aliases = {3: 0}
    return pl.pallas_call(
        _attn_kernel,
        grid=(nb, N_PAIR // pairs, nq),
        in_specs=in_specs,
        out_specs=pl.BlockSpec((pairs, tq, LANES), lambda b, p, i: (p, o_blk0 + b * nq + i, 0)),
        out_shape=jax.ShapeDtypeStruct((N_PAIR, NT, LANES), BF16),
        input_output_aliases=aliases,
        compiler_params=_cparams(("arbitrary", "arbitrary", "arbitrary")),
        name="mla_attention",
    )(*args)


def _ret_kernel(dec_ref, q_ref, k_ref, v_ref, s0_ref, *rest, reverse, finish):
    if finish:
        of_ref, g_ref, gn_ref = rest[:3]
    o_ref, st_ref, s_scr = rest[-3:]
    j = pl.program_id(1)
    d = 1 if reverse else 0
    C = RET_CHUNK

    @pl.when(j == 0)
    def _():
        s_scr[...] = s0_ref[0]

    row = lax.broadcasted_iota(jnp.int32, (C, C), 0).astype(F32)
    col = lax.broadcasted_iota(jnp.int32, (C, C), 1).astype(F32)
    dist = (col - row) if reverse else (row - col)
    live = dist >= 0.0
    rowv = lax.broadcasted_iota(jnp.int32, (C, RET_DV), 0).astype(F32)
    rowk = lax.broadcasted_iota(jnp.int32, (C, RET_DK), 0).astype(F32)
    n_sub = q_ref.shape[0] // C
    for h in range(H_RET):
        dl = jnp.full((1, LANES), dec_ref[d, h], F32)
        lg = jnp.minimum(dl, 0.0) - jnp.log1p(jnp.exp(-jnp.abs(dl)))
        dmask = jnp.where(live, jnp.exp(jnp.where(live, dist, 0.0) * lg), 0.0)
        xi = jnp.exp(((C - rowv) if reverse else (rowv + 1.0)) * lg)
        wk = jnp.exp((rowk if reverse else (C - 1.0 - rowk)) * lg[:, :RET_DK])
        chunk_decay = jnp.exp(C * lg)
        kcols = slice(h * RET_DK, (h + 1) * RET_DK)
        vcols = slice(h * RET_DV, (h + 1) * RET_DV)
        state = s_scr[h]
        for sub in (reversed(range(n_sub)) if reverse else range(n_sub)):
            rows = slice(sub * C, (sub + 1) * C)
            q = q_ref[rows, kcols]
            k = k_ref[rows, kcols]
            v = v_ref[rows, vcols]
            scores = lax.dot_general(q, k, (((1,), (1,)), ((), ())), preferred_element_type=F32) * dmask
            inner = jnp.dot(scores.astype(BF16), v, preferred_element_type=F32)
            cross = jnp.dot(q, state.astype(BF16), preferred_element_type=F32) * xi
            kw = (k.astype(F32) * wk).astype(BF16)
            u = lax.dot_general(kw, v, (((0,), (0,)), ((), ())), preferred_element_type=F32)
            state = chunk_decay * state + u
            o = inner + cross
            if finish:
                o = o + of_ref[rows, vcols]
                mu = jnp.mean(o, axis=-1, keepdims=True)
                var = jnp.mean(jnp.square(o - mu), axis=-1, keepdims=True)
                o = (o - mu) * lax.rsqrt(var + EPS) * gn_ref[:, vcols]
                o = _silu(g_ref[rows, vcols]) * o
            o_ref[rows, vcols] = o.astype(o_ref.dtype)
        s_scr[h] = state
        st_ref[0, h] = state


def _retention_pass(decay, rq, rk, rv, s0, *, nb, nc, reverse, o_fwd=None, gate=None, gn=None, o_prev=None, o_blk0=0):
    C = RET_STEP_ROWS
    chunk = (lambda b, j: (b * nc + (nc - 1 - j), 0)) if reverse else (lambda b, j: (b * nc + j, 0))
    in_specs = [pl.BlockSpec(memory_space=pltpu.SMEM),
                pl.BlockSpec((C, H_RET * RET_DK), chunk),
                pl.BlockSpec((C, H_RET * RET_DK), chunk),
                pl.BlockSpec((C, H_RET * RET_DV), chunk),
                pl.BlockSpec((1, H_RET, RET_DK, RET_DV), lambda b, j: (b, 0, 0, 0))]
    args = [decay, rq, rk, rv, s0]
    finish = o_fwd is not None
    aliases = {}
    if finish:
        in_specs += [pl.BlockSpec((C, H_RET * RET_DV), chunk), pl.BlockSpec((C, H_RET * RET_DV), chunk),
                     _full(gn.shape)]
        args += [o_fwd, gate, gn]
        if o_prev is not None:
            in_specs.append(pl.BlockSpec(memory_space=pl.ANY))
            args.append(o_prev)
            aliases = {len(args) - 1: 0}
        o_spec = pl.BlockSpec((C, H_RET * RET_DV), lambda b, j: (o_blk0 + b * nc + (nc - 1 - j), 0))
        o_shape = jax.ShapeDtypeStruct((NT, H_RET * RET_DV), BF16)
    else:
        o_spec = pl.BlockSpec((C, H_RET * RET_DV), chunk)
        o_shape = jax.ShapeDtypeStruct((nb * nc * C, H_RET * RET_DV), F32)
    return pl.pallas_call(
        functools.partial(_ret_kernel, reverse=reverse, finish=finish),
        grid=(nb, nc),
        in_specs=in_specs,
        out_specs=(o_spec, pl.BlockSpec((1, H_RET, RET_DK, RET_DV), lambda b, j: (b, 0, 0, 0))),
        out_shape=(o_shape, jax.ShapeDtypeStruct((nb, H_RET, RET_DK, RET_DV), F32)),
        scratch_shapes=[pltpu.VMEM((H_RET, RET_DK, RET_DV), F32)],
        input_output_aliases=aliases,
        compiler_params=_cparams(("arbitrary", "arbitrary")),
        name="retention_bwd" if reverse else "retention_fwd",
    )(*args)


def _row_copy(src_ref, s, dst_ref, d, sem):
    return pltpu.make_async_copy(src_ref.at[pl.ds(s, 1), :], dst_ref.at[pl.ds(d, 1), :], sem)


def _wait_rows(src_ref, dst_ref, sem, n):
    def body(r, c):
        _row_copy(src_ref, 0, dst_ref, 0, sem).wait()
        return c
    lax.fori_loop(0, n, body, 0, unroll=ISSUE_UNROLL)


def _residual_and_route(x, y, mod_ref, gffn_ref, wrh_ref, wrl_ref, br_ref, x1_ref, route_ref, cnt_ref, xbuf_ref,
                        cnt_scr, stage_scr, slot_vm, slot_sm, row_sem, slot_sem, late_issue=False):
    i = pl.program_id(0)
    n_steps = pl.num_programs(0)

    @pl.when(i == 0)
    def _():
        cnt_scr[...] = jnp.zeros_like(cnt_scr)

    x1 = x + mod_ref[0, 2:3, :] * y
    x1_ref[...] = x1
    h2 = _prenorm(x1, gffn_ref[...], mod_ref[0, 3:4, :], mod_ref[0, 4:5, :])
    hi = h2.astype(BF16)
    hi32 = hi.astype(F32)
    lo = (h2 - hi32).astype(BF16)
    wrh = wrh_ref[...]
    lg = (jnp.dot(hi, wrh, preferred_element_type=F32) + jnp.dot(lo, wrh, preferred_element_type=F32)
          + jnp.dot(hi, wrl_ref[...], preferred_element_type=F32) + br_ref[...])
    lane = lax.broadcasted_iota(jnp.int32, lg.shape, 1).astype(F32)
    big = float(4 * LANES)
    gl = jnp.where((lane >= N_EXPERTS) & (lane < N_EXPERTS + N_GROUPS), lg, NEG)
    gmax = jnp.max(gl, axis=-1, keepdims=True)
    g_w = 1.0 / jnp.sum(jnp.exp(gl - gmax), axis=-1, keepdims=True)
    g_lane = jnp.min(jnp.where(gl == gmax, lane, big), axis=-1, keepdims=True)
    e_lo = (g_lane - N_EXPERTS) * EXP_PER_GROUP
    el = jnp.where((lane >= e_lo) & (lane < e_lo + EXP_PER_GROUP), lg, NEG)
    m1 = jnp.max(el, axis=-1, keepdims=True)
    i1 = jnp.min(jnp.where(el == m1, lane, big), axis=-1, keepdims=True)
    el2 = jnp.where(lane == i1, NEG, el)
    m2 = jnp.max(el2, axis=-1, keepdims=True)
    i2 = jnp.min(jnp.where(el2 == m2, lane, big), axis=-1, keepdims=True)
    t = jnp.exp(m2 - m1)
    w1 = g_w / (1.0 + t)
    w2 = w1 * t
    oh1 = lane == i1
    oh2 = lane == i2
    oh = jnp.where(oh1 | oh2, 1.0, 0.0)
    rr = lax.broadcasted_iota(jnp.int32, (TM, TM), 0)
    cc = lax.broadcasted_iota(jnp.int32, (TM, TM), 1)
    strict_lower = jnp.where(cc < rr, 1.0, 0.0).astype(BF16)
    before = jnp.dot(strict_lower, oh.astype(BF16), preferred_element_type=F32) + cnt_scr[...]
    slot1 = jnp.sum(jnp.where(oh1, before, 0.0), axis=-1, keepdims=True) + i1 * CAP
    slot2 = jnp.sum(jnp.where(oh2, before, 0.0), axis=-1, keepdims=True) + i2 * CAP
    cnt = cnt_scr[...] + jnp.sum(oh, axis=0, keepdims=True)
    cnt_scr[...] = cnt
    cnt_ref[...] = cnt
    route = jnp.where(lane == 0.0, i1, jnp.where(lane == 1.0, i2, jnp.where(lane == 2.0, w1, jnp.where(
        lane == 3.0, w2, jnp.where(lane == 4.0, slot1, jnp.where(lane == 5.0, slot2, 0.0))))))
    route_ref[...] = route

    disp = (xbuf_ref, stage_scr, slot_vm, slot_sm, row_sem, slot_sem)
    if late_issue:
        _wait_rows(stage_scr.at[0], xbuf_ref, row_sem, 2 * TM)
        _stage_tile(hi32, route, *disp[1:4], slot_sem)

        @pl.when(i == n_steps - 1)
        def _():
            _issue_all_rows(*disp[:4], row_sem)
            _wait_rows(stage_scr.at[0], xbuf_ref, row_sem, 2 * TM)
    else:
        @pl.when(i > 0)
        def _():
            _wait_rows(stage_scr.at[0], xbuf_ref, row_sem, 2 * TM)

        _stage_tile(hi32, route, *disp[1:4], slot_sem)
        _issue_all_rows(*disp[:4], row_sem)

        @pl.when(i == n_steps - 1)
        def _():
            _wait_rows(stage_scr.at[0], xbuf_ref, row_sem, 2 * TM)


def _stage_tile(hi32, route, stage_scr, slot_vm, slot_sm, slot_sem):
    word = (lax.bitcast_convert_type(hi32[:, :PACKED], U32)
            | (lax.bitcast_convert_type(hi32[:, PACKED:], U32) >> 16))
    stage_scr[...] = word.reshape(stage_scr.shape)
    slot_vm[...] = jnp.transpose(route)[0:SUBLANES, :].astype(jnp.int32)
    _slots_to_smem(slot_vm, slot_sm, slot_sem)


def _slots_to_smem(slot_vm, slot_sm, slot_sem):
    to_smem = pltpu.make_async_copy(slot_vm, slot_sm, slot_sem)
    to_smem.start()
    to_smem.wait()


def _issue_row(xbuf_ref, stage_scr, slot_sm, row_sem, g, u):
    src = stage_scr.at[g, pl.ds(u, 1), :]
    r = g * SUBLANES + u
    pltpu.make_async_copy(src, xbuf_ref.at[pl.ds(slot_sm[4, r], 1), :], row_sem).start()
    pltpu.make_async_copy(src, xbuf_ref.at[pl.ds(slot_sm[5, r], 1), :], row_sem).start()


def _issue_all_rows(xbuf_ref, stage_scr, slot_vm, slot_sm, row_sem):
    del slot_vm
    for r in range(TM):
        _issue_row(xbuf_ref, stage_scr, slot_sm, row_sem, r // SUBLANES, r % SUBLANES)


def _prime_late_issue(xbuf_ref, stage_scr, slot_vm, slot_sm, slot_sem):
    stage_scr[...] = jnp.zeros(stage_scr.shape, U32)
    col = lax.broadcasted_iota(jnp.int32, slot_vm.shape, 1)
    row = lax.broadcasted_iota(jnp.int32, slot_vm.shape, 0)
    slot_vm[...] = N_EXPERTS * CAP + col + jnp.where(row == 5, TM, 0)
    _slots_to_smem(slot_vm, slot_sm, slot_sem)


def _tail_specs():
    row = lambda i: (i, 0)
    in_specs = [pl.BlockSpec((1, 6, D_MODEL), lambda i: (_cond_row(i), 0, 0)),
                _full((1, D_MODEL)), _full((D_MODEL, LANES)), _full((D_MODEL, LANES)), _full((1, LANES))]
    out_specs = (pl.BlockSpec((TM, D_MODEL), row), pl.BlockSpec((TM, LANES), row), _full((1, LANES)),
                 pl.BlockSpec(memory_space=pl.ANY))
    out_shape = (jax.ShapeDtypeStruct((NT, D_MODEL), F32), jax.ShapeDtypeStruct((NT, LANES), F32),
                 jax.ShapeDtypeStruct((1, LANES), F32),
                 jax.ShapeDtypeStruct((N_EXPERTS * CAP + 2 * TM, PACKED), U32))
    scratch = [pltpu.VMEM((1, LANES), F32), pltpu.VMEM((TM // SUBLANES, SUBLANES, PACKED), U32),
               pltpu.VMEM((SUBLANES, TM), jnp.int32), pltpu.SMEM((SUBLANES, TM), jnp.int32),
               pltpu.SemaphoreType.DMA(()), pltpu.SemaphoreType.DMA(())]
    return in_specs, out_specs, out_shape, scratch


def _cond_row(i):
    return jnp.where(i < NP_TILES, DEC_BATCH, (i - NP_TILES) // S_TILES_PER_SEQ)


def _ctx_blk(i):
    return jnp.minimum(i, NP_TILES - 1)


def _lat_blk(i):
    return jnp.maximum(i - NP_TILES, 0)


def _even_out_kernel(xp_ref, xs_ref, o_ref, r_ref, wout_ref, *tail):
    y = jnp.dot(r_ref[...], wout_ref[H_MLA * MLA_V:, :], preferred_element_type=F32)
    for p in range(N_PAIR):
        y = y + jnp.dot(o_ref[p], wout_ref[p * LANES:(p + 1) * LANES, :], preferred_element_type=F32)
    x = jnp.where(pl.program_id(0) < NP_TILES, xp_ref[...], xs_ref[...])
    _residual_and_route(x, y, *tail)


def _even_out(xp, xs, o_mla, o_ret, wout, mod, gffn, wrh, wrl, br):
    row = lambda i: (i, 0)
    tail_in, out_specs, out_shape, scratch = _tail_specs()
    return pl.pallas_call(
        _even_out_kernel,
        grid=(NT_TILES,),
        in_specs=[pl.BlockSpec((TM, D_MODEL), lambda i: (_ctx_blk(i), 0)),
                  pl.BlockSpec((TM, D_MODEL), lambda i: (_lat_blk(i), 0)),
                  pl.BlockSpec((N_PAIR, TM, LANES), lambda i: (0, i, 0)),
                  pl.BlockSpec((TM, H_RET * RET_DV), row),
                  _full(wout.shape)] + tail_in,
        out_specs=out_specs,
        out_shape=out_shape,
        scratch_shapes=scratch,
        compiler_params=_cparams(("arbitrary",)),
        name="even_out_route",
    )(xp, xs, o_mla, o_ret, wout, mod, gffn, wrh, wrl, br)


def _conf_in_kernel(x_ref, mod_ref, g_ref, win_ref, u_ref):
    h = _prenorm(x_ref[...], g_ref[...], mod_ref[0, 0:1, :], mod_ref[0, 1:2, :])
    z = jnp.dot(h.astype(BF16), win_ref[...], preferred_element_type=F32)
    u_ref[...] = z[:, :CONV_DIM] * jax.nn.sigmoid(z[:, CONV_DIM:])


def _conf_in(x, mod, gain, win):
    row = lambda i: (i, 0)
    return pl.pallas_call(
        _conf_in_kernel,
        grid=(NT_TILES,),
        in_specs=[pl.BlockSpec((TM, D_MODEL), row),
                  pl.BlockSpec((1, 6, D_MODEL), lambda i: (_cond_row(i), 0, 0)),
                  _full(gain.shape), _full(win.shape)],
        out_specs=pl.BlockSpec((TM, CONV_DIM), row),
        out_shape=jax.ShapeDtypeStruct((NT, CONV_DIM), F32),
        compiler_params=_cparams(("arbitrary",)),
        name="conformer_in_glu",
    )(x, mod, gain, win)


CONV_ROWS = 32
SHIFT_ROWS = TM + 2 * HALO - SUBLANES


def _conv_out_kernel(x_ref, u_ref, ul_ref, ur_ref, dw_ref, dwb_ref, lng_ref, lnb_ref, wout_ref, *tail_and_scratch):
    tail = tail_and_scratch[:-3]
    ext_scr, shift_scr, act_scr = tail_and_scratch[-3:]
    i = pl.program_id(0)
    t = (i - NP_TILES) % S_TILES_PER_SEQ
    first = (i < NP_TILES) | (t == 0)
    last = (i < NP_TILES) | (t == S_TILES_PER_SEQ - 1)
    ext_scr[0:HALO, :] = jnp.where(first, 0.0, ul_ref[...])
    ext_scr[HALO:HALO + TM, :] = u_ref[...]
    ext_scr[HALO + TM:, :] = jnp.where(last, 0.0, ur_ref[...])
    for s in range(SUBLANES):
        shift_scr[s] = ext_scr[pl.ds(s, SHIFT_ROWS), :]
    xbuf_ref, stage_scr, slot_vm, slot_sm, row_sem, slot_sem = tail[8], *tail[10:15]

    @pl.when(i == 0)
    def _():
        _prime_late_issue(xbuf_ref, stage_scr, slot_vm, slot_sm, slot_sem)

    off = HALO - CONV_W // 2
    rows_per_block = CONV_ROWS
    for rb in range(TM // CONV_ROWS):
        acc = jnp.zeros((CONV_ROWS, CONV_DIM), F32)
        for kk in range(CONV_W):
            s, a = (off + kk) % SUBLANES, (off + kk) // SUBLANES * SUBLANES
            acc = acc + shift_scr[s, pl.ds(rb * CONV_ROWS + a, CONV_ROWS), :] * dw_ref[kk:kk + 1, :]
            for r in range(rb * rows_per_block + kk * rows_per_block // CONV_W,
                           rb * rows_per_block + (kk + 1) * rows_per_block // CONV_W):
                _issue_row(xbuf_ref, stage_scr, slot_sm, row_sem, r // SUBLANES, r % SUBLANES)
        c = acc + dwb_ref[...]
        mu = jnp.mean(c, axis=-1, keepdims=True)
        var = jnp.mean(jnp.square(c - mu), axis=-1, keepdims=True)
        c = (c - mu) * lax.rsqrt(var + EPS) * lng_ref[...] + lnb_ref[...]
        act_scr[rb * CONV_ROWS:(rb + 1) * CONV_ROWS, :] = _silu(c).astype(BF16)
    y = jnp.dot(act_scr[...], wout_ref[...], preferred_element_type=F32)
    _residual_and_route(x_ref[...], y, *tail, late_issue=True)


def _conv_out(x, u, dw, dwb, lng, lnb, wout, mod, gffn, wrh, wrl, br):
    row = lambda i: (i, 0)
    per = TM // HALO
    n_halo = NT // HALO
    tail_in, out_specs, out_shape, scratch = _tail_specs()
    return pl.pallas_call(
        _conv_out_kernel,
        grid=(NT_TILES,),
        in_specs=[pl.BlockSpec((TM, D_MODEL), row),
                  pl.BlockSpec((TM, CONV_DIM), row),
                  pl.BlockSpec((HALO, CONV_DIM), lambda i: (jnp.maximum(i * per - 1, 0), 0)),
                  pl.BlockSpec((HALO, CONV_DIM), lambda i: (jnp.minimum((i + 1) * per, n_halo - 1), 0)),
                  _full(dw.shape), _full(dwb.shape), _full(lng.shape), _full(lnb.shape), _full(wout.shape)] + tail_in,
        out_specs=out_specs,
        out_shape=out_shape,
        scratch_shapes=scratch + [pltpu.VMEM((TM + 2 * HALO, CONV_DIM), F32),
                                  pltpu.VMEM((SUBLANES, SHIFT_ROWS, CONV_DIM), F32),
                                  pltpu.VMEM((TM, CONV_DIM), BF16)],
        compiler_params=_cparams(("arbitrary",)),
        name="conformer_conv_out_route",
    )(x, u, u, u, dw, dwb, lng, lnb, wout, mod, gffn, wrh, wrl, br)


def _pad_fill_kernel(cnt_ref, xin_ref, xbuf_ref, zero_scr, sem):
    del xin_ref
    e = pl.program_id(0)
    n = cnt_ref[e]
    pad = (EB - n % EB) % EB
    zero_scr[...] = jnp.zeros_like(zero_scr)

    def start(r, c):
        _row_copy(zero_scr, 0, xbuf_ref, e * CAP + n + r, sem).start()
        return c

    def wait(r, c):
        _row_copy(zero_scr, 0, xbuf_ref, e * CAP + n + r, sem).wait()
        return c

    lax.fori_loop(0, pad, start, 0)
    lax.fori_loop(0, pad, wait, 0)


def _pad_fill(counts, xbuf):
    return pl.pallas_call(
        _pad_fill_kernel,
        grid_spec=pltpu.PrefetchScalarGridSpec(
            num_scalar_prefetch=1,
            grid=(N_EXPERTS,),
            in_specs=[pl.BlockSpec(memory_space=pl.ANY)],
            out_specs=pl.BlockSpec(memory_space=pl.ANY),
            scratch_shapes=[pltpu.VMEM((SUBLANES, PACKED), U32), pltpu.SemaphoreType.DMA(())]),
        out_shape=jax.ShapeDtypeStruct(xbuf.shape, xbuf.dtype),
        input_output_aliases={1: 0},
        compiler_params=_cparams(("arbitrary",)),
        name="moe_pad_fill",
    )(counts, xbuf)


def _expert_kernel(blk_ref, exp_ref, flag_ref, x_ref, wu_ref, wd_ref, o_ref, wu_scr, wd_scr):
    i = pl.program_id(0)
    flag = flag_ref[i]

    @pl.when((flag & 2) != 0)
    def _():
        wu_scr[...] = wu_ref[0, 0].astype(BF16)
        wd_scr[...] = wd_ref[0, 0].astype(BF16)

    @pl.when((flag & 1) != 0)
    def _():
        w = x_ref[...]
        xa = lax.bitcast_convert_type(w & jnp.uint32(0xFFFF0000), F32).astype(BF16)
        xb = lax.bitcast_convert_type(w << 16, F32).astype(BF16)
        ab = (jnp.dot(xa, wu_scr[:PACKED, :], preferred_element_type=F32)
              + jnp.dot(xb, wu_scr[PACKED:, :], preferred_element_type=F32))
        mid = (_silu(ab[:, :D_EXPERT]) * ab[:, D_EXPERT:]).astype(BF16)
        o_ref[...] = jnp.dot(mid, wd_scr[...], preferred_element_type=F32)


def _experts(blk, exp, flag, xbuf, w_up, w_down, layer):
    return pl.pallas_call(
        _expert_kernel,
        grid_spec=pltpu.PrefetchScalarGridSpec(
            num_scalar_prefetch=3,
            grid=(N_ITEMS,),
            in_specs=[pl.BlockSpec((EB, PACKED), lambda i, b, e, f: (b[i], 0)),
                      pl.BlockSpec((1, 1, D_MODEL, 2 * D_EXPERT), lambda i, b, e, f: (layer, e[i], 0, 0)),
                      pl.BlockSpec((1, 1, D_EXPERT, D_MODEL), lambda i, b, e, f: (layer, e[i], 0, 0))],
            out_specs=pl.BlockSpec((EB, D_MODEL), lambda i, b, e, f: (b[i], 0)),
            scratch_shapes=[pltpu.VMEM((D_MODEL, 2 * D_EXPERT), BF16), pltpu.VMEM((D_EXPERT, D_MODEL), BF16)]),
        out_shape=jax.ShapeDtypeStruct((N_EXPERTS * CAP, D_MODEL), F32),
        compiler_params=_cparams(("arbitrary",)),
        name="moe_experts",
    )(blk, exp, flag, xbuf, w_up, w_down)


def _combine_kernel(slot_ref, x_ref, route_ref, mod_ref, ybuf_ref, *rest):
    outs, (g_scr, sem) = rest[:-2], rest[-2:]
    i = pl.program_id(0)
    n_steps = pl.num_programs(0)

    def gather(tile, buf):
        base = tile * (2 * TM)
        for r in range(TM):
            _row_copy(ybuf_ref, slot_ref[base + 2 * r], g_scr.at[buf], r, sem.at[buf]).start()
            _row_copy(ybuf_ref, slot_ref[base + 2 * r + 1], g_scr.at[buf], TM + r, sem.at[buf]).start()

    cur = i % 2

    @pl.when(i == 0)
    def _():
        gather(0, 0)

    for parity in range(2):
        @pl.when((i + 1 < n_steps) & (cur == parity))
        def _():
            gather(i + 1, 1 - parity)

    _wait_rows(ybuf_ref, g_scr.at[cur], sem.at[cur], 2 * TM)
    route = route_ref[...]
    y = route[:, 2:3] * g_scr[cur, 0:TM, :] + route[:, 3:4] * g_scr[cur, TM:2 * TM, :]
    res = x_ref[...] + mod_ref[0, 5:6, :] * y
    if len(outs) == 1:
        outs[0][...] = res
    else:
        @pl.when(i < NP_TILES)
        def _():
            outs[0][...] = res

        @pl.when(i >= NP_TILES)
        def _():
            outs[1][...] = res


def _combine(slots, x1, route, mod, ybuf, *, split):
    row = lambda i, s: (i, 0)
    if split:
        out_specs = (pl.BlockSpec((TM, D_MODEL), lambda i, s: (_ctx_blk(i), 0)),
                     pl.BlockSpec((TM, D_MODEL), lambda i, s: (_lat_blk(i), 0)))
        out_shape = (jax.ShapeDtypeStruct((NP, D_MODEL), F32), jax.ShapeDtypeStruct((NS, D_MODEL), F32))
    else:
        out_specs = pl.BlockSpec((TM, D_MODEL), row)
        out_shape = jax.ShapeDtypeStruct((NT, D_MODEL), F32)
    return pl.pallas_call(
        _combine_kernel,
        grid_spec=pltpu.PrefetchScalarGridSpec(
            num_scalar_prefetch=1,
            grid=(NT_TILES,),
            in_specs=[pl.BlockSpec((TM, D_MODEL), row),
                      pl.BlockSpec((TM, LANES), row),
                      pl.BlockSpec((1, 6, D_MODEL), lambda i, s: (_cond_row(i), 0, 0)),
                      pl.BlockSpec(memory_space=pl.ANY)],
            out_specs=out_specs,
            scratch_shapes=[pltpu.VMEM((2, 2 * TM, D_MODEL), F32), pltpu.SemaphoreType.DMA((2,))]),
        out_shape=out_shape,
        compiler_params=_cparams(("arbitrary",)),
        name="moe_combine",
    )(slots, x1, route, mod, ybuf)


def _moe(x1, xbuf, route, counts, mod, w_up, w_down, layer, *, split):
    slots = route[:, 4:6].astype(jnp.int32).reshape(-1)
    cnt = counts[0, :N_EXPERTS].astype(jnp.int32)
    nblk = (cnt + EB - 1) // EB
    ends = jnp.cumsum(nblk)
    total = ends[-1]
    item = jnp.arange(N_ITEMS, dtype=jnp.int32)
    valid = item < total
    item_c = jnp.minimum(item, total - 1)
    exp = jnp.minimum(jnp.sum((item_c[:, None] >= ends[None, :]).astype(jnp.int32), axis=1), N_EXPERTS - 1)
    j = item_c - (ends[exp] - nblk[exp])
    blk = (exp * CAP_BLOCKS + j).astype(jnp.int32)
    flag = (valid.astype(jnp.int32) + 2 * (valid & (j == 0)).astype(jnp.int32)).astype(jnp.int32)

    xbuf = _pad_fill(cnt, xbuf)
    ybuf = _experts(blk, exp, flag, xbuf, w_up, w_down, layer)
    return _combine(slots, x1, route, mod, ybuf, split=split)


def _pad_heads(w, width, real):
    lead = w.shape[:-1]
    w = w.reshape(lead + (H_MLA, real))
    w = jnp.pad(w, [(0, 0)] * len(lead) + [(0, 0), (0, width - real)])
    return w.reshape(lead + (H_MLA * width,))


def _rope_tables():
    L = DEC_SEQ
    rows = L // GRID_W
    r = jnp.repeat(jnp.arange(rows, dtype=F32), GRID_W)
    col = jnp.tile(jnp.arange(GRID_W, dtype=F32), rows)
    n_f = ROPE_DIM // 4
    freqs = ROPE_BASE ** (-jnp.arange(n_f, dtype=F32) / n_f)
    ang = jnp.concatenate([r[:, None] * freqs, col[:, None] * freqs], axis=-1)
    cos, sin = jnp.cos(ang), jnp.sin(ang)
    half = ROPE_DIM // 2
    z = lambda n: jnp.zeros((L, n), F32)
    o = lambda n: jnp.ones((L, n), F32)
    cos_t = jnp.concatenate([o(MLA_NOPE), cos, cos, o(LANES - MLA_QK)], axis=-1)
    sin_lo = jnp.concatenate([z(MLA_NOPE), -sin, z(half), z(LANES - MLA_QK)], axis=-1)
    sin_hi = jnp.concatenate([z(MLA_NOPE), z(half), sin, z(LANES - MLA_QK)], axis=-1)
    ident = (jnp.ones((TM, LANES), F32), jnp.zeros((TM, LANES), F32), jnp.zeros((TM, LANES), F32))
    return (cos_t, sin_lo, sin_hi), ident


def _even_weights(w_in, q_a_norm, w_q_up, kv_a_norm, w_kv_up, q_norm, k_norm):
    d = D_MODEL
    kr_cols = jnp.zeros((d, LANES), F32).at[:, MLA_NOPE:MLA_QK].set(w_in[:, Q_RANK + KV_RANK:Q_RANK + KV_RANK + ROPE_DIM])
    rest = w_in[:, Q_RANK + KV_RANK + ROPE_DIM:]
    win = jnp.concatenate([w_in[:, :Q_RANK + KV_RANK], kr_cols, rest], axis=-1).astype(BF16)
    wq = _pad_heads(w_q_up, LANES, MLA_QK).astype(BF16)
    kv = w_kv_up.reshape(KV_RANK, H_MLA, MLA_NOPE + MLA_V)
    wk = _pad_heads(kv[:, :, :MLA_NOPE].reshape(KV_RANK, H_MLA * MLA_NOPE), LANES, MLA_NOPE).astype(BF16)
    wv = kv[:, :, MLA_NOPE:].reshape(KV_RANK, H_MLA * MLA_V).astype(BF16)
    padg = lambda g: jnp.pad(g, (0, LANES - MLA_QK)).reshape(1, LANES)
    qg = padg(q_norm) * (MLA_QK ** -0.5)
    kg = padg(k_norm)
    return (win, q_a_norm.reshape(1, -1), wq, kv_a_norm.reshape(1, -1), wk, wv, qg, kg)


def _router_weights(w_group, b_group, w_expert, b_expert):
    w = jnp.zeros((D_MODEL, LANES), F32).at[:, :N_EXPERTS].set(w_expert).at[:, N_EXPERTS:N_EXPERTS + N_GROUPS].set(w_group)
    b = jnp.zeros((1, LANES), F32).at[0, :N_EXPERTS].set(b_expert).at[0, N_EXPERTS:N_EXPERTS + N_GROUPS].set(b_group)
    hi = w.astype(BF16)
    lo = (w - hi.astype(F32)).astype(BF16)
    return hi, lo, b


def kernel(x_prompt, x_sample, cache_mla_ckv, cache_mla_krope, state_retention, c, c_ctx, ada_w, ada_b, norm_mix, norm_ffn, ev_w_in, ev_q_a_norm, ev_w_q_up, ev_kv_a_norm, ev_w_kv_up, ev_q_norm, ev_k_norm, ev_ret_decay, ev_ret_gn, ev_w_out, od_w_in, od_dw, od_dw_b, od_ln_g, od_ln_b, od_w_out, moe_w_group, moe_b_group, moe_w_expert, moe_b_expert, moe_w_up, moe_w_down):
    depth = ada_w.shape[0]
    x, xp, xs = None, x_prompt.reshape(NP, D_MODEL), x_sample.reshape(NS, D_MODEL)
    cond = jnp.concatenate([c, c_ctx[None, :], jnp.zeros((2 * SUBLANES - DEC_BATCH - 1, D_MODEL), F32)], axis=0)
    mods = _ada(cond, ada_w, ada_b).reshape(depth, cond.shape[0], 6, D_MODEL)
    rope, rope_ident = _rope_tables()
    new_ckv, new_krope, new_state = [], [], []

    for l in range(depth):
        mod = mods[l]
        jj = l // 2
        router = _router_weights(moe_w_group[l], moe_b_group[l], moe_w_expert[l], moe_b_expert[l])
        gmix = norm_mix[l].reshape(1, D_MODEL)
        gffn = norm_ffn[l].reshape(1, D_MODEL)
        if l % 2 == 0:
            wts = _even_weights(ev_w_in[jj], ev_q_a_norm[jj], ev_w_q_up[jj], ev_kv_a_norm[jj], ev_w_kv_up[jj],
                                ev_q_norm[jj], ev_k_norm[jj])
            if xp is None:
                xp, xs = x[:NP], x[NP:]
            qp, kp, vp, ckv_p, kr_p, rq_p, rk_p, rv_p, rg_p = _even_in(
                xp, mod, gmix, wts, rope_ident, n_tiles=NP_TILES, x_blk=lambda i: i, mod_row=lambda i: DEC_BATCH,
                rope_blk=lambda i: 0, kv_rows=NP, kv_blk=lambda i: i)
            qs, ks, vs, _, _, rq_s, rk_s, rv_s, rg_s = _even_in(
                xs, mod, gmix, wts, rope, n_tiles=NS // TM, x_blk=lambda i: i,
                mod_row=lambda i: i // S_TILES_PER_SEQ, rope_blk=lambda i: i % S_TILES_PER_SEQ,
                kv_rows=DEC_BATCH * LK_S,
                kv_blk=lambda i: (i // S_TILES_PER_SEQ) * KV_BLOCKS_PER_SEQ + 1 + i % S_TILES_PER_SEQ)
            kr_cache = jnp.pad(cache_mla_krope[:, jj].reshape(DEC_BATCH * PAST_LEN, ROPE_DIM),
                               ((0, 0), (MLA_NOPE, LANES - MLA_QK)))
            ks, vs = _ctx_kv(cache_mla_ckv[:, jj].reshape(DEC_BATCH * PAST_LEN, KV_RANK), kr_cache,
                             wts[4], wts[5], wts[7], rope_ident, ks, vs)
            o_mla = _attention(qp, kp, vp, None, nb=BATCH, tq=SEQ, lk=SEQ, o_row0=0, pairs=N_PAIR)
            o_mla = _attention(qs, ks, vs, o_mla, nb=DEC_BATCH, tq=TQ_LATENT, lk=LK_S, o_row0=NP, pairs=N_PAIR)

            decay = ev_ret_decay[jj]
            gn = ev_ret_gn[jj].reshape(1, -1)
            zero_state = jnp.zeros((BATCH, H_RET, RET_DK, RET_DV), F32)
            ncp, ncs = SEQ // RET_STEP_ROWS, DEC_SEQ // RET_STEP_ROWS
            of_p, sf_p = _retention_pass(decay, rq_p, rk_p, rv_p, zero_state, nb=BATCH, nc=ncp, reverse=False)
            o_ret, sb_p = _retention_pass(decay, rq_p, rk_p, rv_p, zero_state, nb=BATCH, nc=ncp, reverse=True,
                                          o_fwd=of_p, gate=rg_p, gn=gn)
            of_s, _ = _retention_pass(decay, rq_s, rk_s, rv_s, state_retention[:, jj, 0], nb=DEC_BATCH, nc=ncs,
                                      reverse=False)
            o_ret, _ = _retention_pass(decay, rq_s, rk_s, rv_s, state_retention[:, jj, 1], nb=DEC_BATCH, nc=ncs,
                                       reverse=True, o_fwd=of_s, gate=rg_s, gn=gn, o_prev=o_ret,
                                       o_blk0=NP // RET_STEP_ROWS)
            new_ckv.append(ckv_p.reshape(BATCH, SEQ, KV_RANK))
            new_krope.append(kr_p[:, MLA_NOPE:MLA_QK].reshape(BATCH, SEQ, ROPE_DIM))
            new_state.append(jnp.stack([sf_p, sb_p], axis=1))
            x1, route, counts, xbuf = _even_out(xp, xs, o_mla, o_ret, ev_w_out[jj].astype(BF16), mod, gffn, *router)
        else:
            x = jnp.concatenate([xp, xs], axis=0) if x is None else x
            u = _conf_in(x, mod, gmix, od_w_in[jj].astype(BF16))
            x1, route, counts, xbuf = _conv_out(
                x, u, od_dw[jj], od_dw_b[jj].reshape(1, -1), od_ln_g[jj].reshape(1, -1), od_ln_b[jj].reshape(1, -1),
                od_w_out[jj].astype(BF16), mod, gffn, *router)
        if l == depth - 1:
            xp, xs = _moe(x1, xbuf, route, counts, mod, moe_w_up, moe_w_down, l, split=True)
            x = None
        else:
            x = _moe(x1, xbuf, route, counts, mod, moe_w_up, moe_w_down, l, split=False)
            xp, xs = None, None

    return (xp.reshape(BATCH, SEQ, D_MODEL), xs.reshape(DEC_BATCH, DEC_SEQ, D_MODEL),
            jnp.stack(new_ckv, axis=1), jnp.stack(new_krope, axis=1), jnp.stack(new_state, axis=1))
```

```python
import functools

import jax
import jax.numpy as jnp
from jax import lax
from jax.experimental import pallas as pl
from jax.experimental.pallas import tpu as pltpu

F32 = jnp.float32
BF16 = jnp.bfloat16
U32 = jnp.uint32

D_MODEL = 1024
BATCH, SEQ = 32, 256
DEC_BATCH, DEC_SEQ = 8, 4096
PAST_LEN = 256
GRID_W = 64
EPS = 1e-6
H_MLA, Q_RANK, KV_RANK = 8, 256, 128
MLA_NOPE, ROPE_DIM, MLA_V = 64, 32, 64
MLA_QK = MLA_NOPE + ROPE_DIM
ROPE_BASE = 10000.0
H_RET, RET_DK, RET_DV, RET_CHUNK = 4, 64, 128, 128
CONV_DIM, CONV_W = 1024, 31
N_GROUPS, EXP_PER_GROUP, D_EXPERT = 4, 8, 256
N_EXPERTS = N_GROUPS * EXP_PER_GROUP

LANES = 128
SUBLANES = 8
VMEM_LIMIT = 48 * 1024 * 1024

NP = BATCH * SEQ
NS = DEC_BATCH * DEC_SEQ
NT = NP + NS
TM = 256
NT_TILES = NT // TM
NP_TILES = NP // TM
S_TILES_PER_SEQ = DEC_SEQ // TM
LK_S = PAST_LEN + DEC_SEQ
KV_BLOCKS_PER_SEQ = LK_S // TM
N_PAIR = H_MLA // 2
TQ_LATENT = 256
TQ_CHAIN = 256
RET_STEP_ROWS = 2 * RET_CHUNK
EB = 256
CAP = NT + EB
CAP_BLOCKS = CAP // EB
N_ITEMS = (2 * NT) // EB + N_EXPERTS
HALO = 16
PACKED = D_MODEL // 2
ISSUE_UNROLL = 8
NEG = -1e30


def _cparams(sem):
    return pltpu.CompilerParams(dimension_semantics=sem, vmem_limit_bytes=VMEM_LIMIT)


def _full(shape):
    n = len(shape)
    return pl.BlockSpec(shape, lambda *_: (0,) * n)


def _rms(x, gain):
    return x * lax.rsqrt(jnp.mean(x * x, axis=-1, keepdims=True) + EPS) * gain


def _prenorm(x, gain, shift, scale):
    return _rms(x, gain) * (1.0 + scale) + shift


def _silu(x):
    return x * jax.nn.sigmoid(x)


def _rope128(x, cos, sin_lo, sin_hi):
    return x * cos + pltpu.roll(x, LANES - ROPE_DIM // 2, 1) * sin_lo + pltpu.roll(x, ROPE_DIM // 2, 1) * sin_hi


def _head_norm_rope(slab, gain, cos, sin_lo, sin_hi):
    r = lax.rsqrt(jnp.sum(slab * slab, axis=-1, keepdims=True) * (1.0 / MLA_QK) + EPS)
    return _rope128(slab * r * gain, cos, sin_lo, sin_hi)


def _ada_kernel(c_ref, w_ref, b_ref, o_ref):
    s = _silu(c_ref[...]).astype(BF16)
    o_ref[0] = jnp.dot(s, w_ref[0].astype(BF16), preferred_element_type=F32) + b_ref[0]


def _ada(cond, ada_w, ada_b):
    depth, d, n = ada_w.shape
    rows = cond.shape[0]
    tn = 1536
    return pl.pallas_call(
        _ada_kernel,
        grid=(depth, n // tn),
        in_specs=[pl.BlockSpec((rows, d), lambda l, j: (0, 0)),
                  pl.BlockSpec((1, d, tn), lambda l, j: (l, 0, j)),
                  pl.BlockSpec((1, 1, tn), lambda l, j: (l, 0, j))],
        out_specs=pl.BlockSpec((1, rows, tn), lambda l, j: (l, 0, j)),
        out_shape=jax.ShapeDtypeStruct((depth, rows, n), F32),
        compiler_params=_cparams(("arbitrary", "arbitrary")),
        name="ada_modulation",
    )(cond, ada_w, ada_b.reshape(depth, 1, n))


def _kv_heads(ckvn_bf, kr_slab, wk_ref, wv_ref, kg, cos, sin_lo, sin_hi, k_ref, v_ref):
    kk = jnp.dot(ckvn_bf, wk_ref[...], preferred_element_type=F32)
    vv = jnp.dot(ckvn_bf, wv_ref[...], preferred_element_type=F32)
    for h in range(H_MLA):
        kh = kk[:, h * LANES:(h + 1) * LANES] + kr_slab
        k_ref[h] = _head_norm_rope(kh, kg, cos, sin_lo, sin_hi).astype(BF16)
    for j in range(N_PAIR):
        v_ref[j] = vv[:, j * LANES:(j + 1) * LANES].astype(BF16)


def _even_in_kernel(x_ref, mod_ref, g_ref, win_ref, qan_ref, wq_ref, kvan_ref, wk_ref, wv_ref, qg_ref, kg_ref,
                    cos_ref, sl_ref, sh_ref,
                    q_ref, k_ref, v_ref, ckv_ref, kr_ref, rq_ref, rk_ref, rv_ref, rg_ref):
    h = _prenorm(x_ref[...], g_ref[...], mod_ref[0, 0:1, :], mod_ref[0, 1:2, :])
    z = jnp.dot(h.astype(BF16), win_ref[...], preferred_element_type=F32)
    cq = z[:, 0:256]
    ckv = z[:, 256:384]
    kr_slab = z[:, 384:512]
    cos, sin_lo, sin_hi = cos_ref[...], sl_ref[...], sh_ref[...]

    qq = jnp.dot(_rms(cq, qan_ref[...]).astype(BF16), wq_ref[...], preferred_element_type=F32)
    qg = qg_ref[...]
    for hh in range(H_MLA):
        q_ref[hh] = _head_norm_rope(qq[:, hh * LANES:(hh + 1) * LANES], qg, cos, sin_lo, sin_hi).astype(BF16)

    ckvn = _rms(ckv, kvan_ref[...])
    ckv_ref[...] = ckvn
    kr_ref[...] = kr_slab
    _kv_heads(ckvn.astype(BF16), kr_slab, wk_ref, wv_ref, kg_ref[...], cos, sin_lo, sin_hi, k_ref, v_ref)

    rq_ref[...] = z[:, 512:768].astype(BF16)
    rk_ref[...] = (z[:, 768:1024] * (RET_DK ** -0.5)).astype(BF16)
    rv_ref[...] = z[:, 1024:1536].astype(BF16)
    rg_ref[...] = z[:, 1536:2048]


def _even_in(x, mod, gain, wts, rope, *, n_tiles, x_blk, mod_row, rope_blk, kv_rows, kv_blk):
    win, qan, wq, kvan, wk, wv, qg, kg = wts
    cos, sin_lo, sin_hi = rope
    ntok = n_tiles * TM
    row = lambda f: (lambda i: (f(i), 0))
    tab = pl.BlockSpec((TM, LANES), row(rope_blk))
    heads = lambda f: (lambda i: (0, f(i), 0))
    out_shape = (
        jax.ShapeDtypeStruct((H_MLA, ntok, LANES), BF16),
        jax.ShapeDtypeStruct((H_MLA, kv_rows, LANES), BF16),
        jax.ShapeDtypeStruct((N_PAIR, kv_rows, LANES), BF16),
        jax.ShapeDtypeStruct((ntok, KV_RANK), F32),
        jax.ShapeDtypeStruct((ntok, LANES), F32),
        jax.ShapeDtypeStruct((ntok, H_RET * RET_DK), BF16),
        jax.ShapeDtypeStruct((ntok, H_RET * RET_DK), BF16),
        jax.ShapeDtypeStruct((ntok, H_RET * RET_DV), BF16),
        jax.ShapeDtypeStruct((ntok, H_RET * RET_DV), F32),
    )
    ident = lambda i: i
    out_specs = (
        pl.BlockSpec((H_MLA, TM, LANES), heads(ident)),
        pl.BlockSpec((H_MLA, TM, LANES), heads(kv_blk)),
        pl.BlockSpec((N_PAIR, TM, LANES), heads(kv_blk)),
        pl.BlockSpec((TM, KV_RANK), row(ident)),
        pl.BlockSpec((TM, LANES), row(ident)),
        pl.BlockSpec((TM, H_RET * RET_DK), row(ident)),
        pl.BlockSpec((TM, H_RET * RET_DK), row(ident)),
        pl.BlockSpec((TM, H_RET * RET_DV), row(ident)),
        pl.BlockSpec((TM, H_RET * RET_DV), row(ident)),
    )
    return pl.pallas_call(
        _even_in_kernel,
        grid=(n_tiles,),
        in_specs=[pl.BlockSpec((TM, D_MODEL), row(x_blk)),
                  pl.BlockSpec((1, 6, D_MODEL), lambda i: (mod_row(i), 0, 0)),
                  _full(gain.shape), _full(win.shape), _full(qan.shape), _full(wq.shape), _full(kvan.shape),
                  _full(wk.shape), _full(wv.shape), _full(qg.shape), _full(kg.shape), tab, tab, tab],
        out_specs=out_specs,
        out_shape=out_shape,
        compiler_params=_cparams(("arbitrary",)),
        name="even_in_proj",
    )(x, mod, gain, win, qan, wq, kvan, wk, wv, qg, kg, cos, sin_lo, sin_hi)


def _ctx_kv_kernel(ckv_ref, kr_ref, wk_ref, wv_ref, kg_ref, cos_ref, sl_ref, sh_ref, k_in, v_in, k_ref, v_ref):
    del k_in, v_in
    _kv_heads(ckv_ref[...].astype(BF16), kr_ref[...], wk_ref, wv_ref, kg_ref[...],
              cos_ref[...], sl_ref[...], sh_ref[...], k_ref, v_ref)


def _ctx_kv(cache_ckv, cache_kr_slab, wk, wv, kg, rope_ident, k_all, v_all):
    cos, sin_lo, sin_hi = rope_ident
    nb = cache_ckv.shape[0] // PAST_LEN
    blk = lambda b: (0, b * KV_BLOCKS_PER_SEQ, 0)
    any_spec = pl.BlockSpec(memory_space=pl.ANY)
    return pl.pallas_call(
        _ctx_kv_kernel,
        grid=(nb,),
        in_specs=[pl.BlockSpec((PAST_LEN, KV_RANK), lambda b: (b, 0)),
                  pl.BlockSpec((PAST_LEN, LANES), lambda b: (b, 0)),
                  _full(wk.shape), _full(wv.shape), _full(kg.shape),
                  _full(cos.shape), _full(cos.shape), _full(cos.shape), any_spec, any_spec],
        out_specs=(pl.BlockSpec((H_MLA, PAST_LEN, LANES), blk), pl.BlockSpec((N_PAIR, PAST_LEN, LANES), blk)),
        out_shape=(jax.ShapeDtypeStruct(k_all.shape, k_all.dtype), jax.ShapeDtypeStruct(v_all.shape, v_all.dtype)),
        input_output_aliases={8: 0, 9: 1},
        compiler_params=_cparams(("arbitrary",)),
        name="ctx_kv_heads",
    )(cache_ckv, cache_kr_slab, wk, wv, kg, cos, sin_lo, sin_hi, k_all, v_all)


def _attn_kernel(q_ref, k_ref, v_ref, *rest):
    o_ref = rest[-1]
    chain_rows = min(TQ_CHAIN, q_ref.shape[1])
    for qt in range(q_ref.shape[1] // chain_rows):
        rows = slice(qt * chain_rows, (qt + 1) * chain_rows)
        for pr in range(v_ref.shape[0]):
            v = v_ref[pr]
            outs = []
            for a in range(2):
                h = 2 * pr + a
                s = lax.dot_general(q_ref[h, rows, :], k_ref[h], (((1,), (1,)), ((), ())),
                                    preferred_element_type=F32)
                p = jnp.exp(s - jnp.max(s, axis=-1, keepdims=True))
                l = jnp.sum(p, axis=-1, keepdims=True)
                outs.append(jnp.dot(p.astype(BF16), v, preferred_element_type=F32) / l)
            lane = lax.broadcasted_iota(jnp.int32, outs[0].shape, 1)
            o_ref[pr, rows, :] = jnp.where(lane < MLA_V, outs[0], outs[1]).astype(BF16)


def _attention(q, k, v, o_prev, *, nb, tq, lk, o_row0, pairs):
    nq = q.shape[1] // (nb * tq)
    o_blk0 = o_row0 // tq
    in_specs = [pl.BlockSpec((2 * pairs, tq, LANES), lambda b, p, i: (p, b * nq + i, 0)),
                pl.BlockSpec((2 * pairs, lk, LANES), lambda b, p, i: (p, b, 0)),
                pl.BlockSpec((pairs, lk, LANES), lambda b, p, i: (p, b, 0))]
    args = [q, k, v]
    aliases = {}
    if o_prev is not None:
        in_specs.append(pl.BlockSpec(memory_space=pl.ANY))
        args.append(o_prev)
        aliases = {3: 0}
    return pl.pallas_call(
        _attn_kernel,
        grid=(nb, N_PAIR // pairs, nq),
        in_specs=in_specs,
        out_specs=pl.BlockSpec((pairs, tq, LANES), lambda b, p, i: (p, o_blk0 + b * nq + i, 0)),
        out_shape=jax.ShapeDtypeStruct((N_PAIR, NT, LANES), BF16),
        input_output_aliases=aliases,
        compiler_params=_cparams(("arbitrary", "arbitrary", "arbitrary")),
        name="mla_attention",
    )(*args)


def _log_sigmoid_lanes(x):
    dl = jnp.full((1, LANES), x, F32)
    return jnp.minimum(dl, 0.0) - jnp.log1p(jnp.exp(-jnp.abs(dl)))


def _ret_bwd_state_kernel(dec_ref, k_ref, v_ref, s0_ref, t_ref, st_ref, s_scr):
    C = RET_CHUNK

    @pl.when(pl.program_id(1) == 0)
    def _():
        s_scr[...] = s0_ref[0]

    rowk = lax.broadcasted_iota(jnp.int32, (C, RET_DK), 0).astype(F32)
    n_sub = k_ref.shape[0] // C
    for h in range(H_RET):
        lg = _log_sigmoid_lanes(dec_ref[1, h])
        wk = jnp.exp(rowk * lg[:, :RET_DK])
        chunk_decay = jnp.exp(C * lg)
        state = s_scr[h]
        for sub in reversed(range(n_sub)):
            rows = slice(sub * C, (sub + 1) * C)
            t_ref[sub, h] = state
            kw = (k_ref[rows, h * RET_DK:(h + 1) * RET_DK].astype(F32) * wk).astype(BF16)
            u = lax.dot_general(kw, v_ref[rows, h * RET_DV:(h + 1) * RET_DV], (((0,), (0,)), ((), ())),
                                preferred_element_type=F32)
            state = chunk_decay * state + u
        s_scr[h] = state
        st_ref[0, h] = state


def _ret_main_kernel(dec_ref, q_ref, k_ref, v_ref, s0_ref, t_ref, g_ref, gn_ref, *rest):
    o_ref, st_ref, s_scr = rest[-3:]
    C = RET_CHUNK

    @pl.when(pl.program_id(1) == 0)
    def _():
        s_scr[...] = s0_ref[0]

    row = lax.broadcasted_iota(jnp.int32, (C, C), 0).astype(F32)
    col = lax.broadcasted_iota(jnp.int32, (C, C), 1).astype(F32)
    dist = row - col
    rowv = lax.broadcasted_iota(jnp.int32, (C, RET_DV), 0).astype(F32)
    rowk = lax.broadcasted_iota(jnp.int32, (C, RET_DK), 0).astype(F32)
    n_sub = q_ref.shape[0] // C
    for h in range(H_RET):
        lg_f = _log_sigmoid_lanes(dec_ref[0, h])
        lg_b = _log_sigmoid_lanes(dec_ref[1, h])
        dmask = (jnp.where(dist >= 0.0, jnp.exp(jnp.maximum(dist, 0.0) * lg_f), 0.0)
                 + jnp.where(dist <= 0.0, jnp.exp(jnp.maximum(-dist, 0.0) * lg_b), 0.0))
        xi_f = jnp.exp((rowv + 1.0) * lg_f)
        xi_b = jnp.exp((C - rowv) * lg_b)
        wk_f = jnp.exp((C - 1.0 - rowk) * lg_f[:, :RET_DK])
        decay_f = jnp.exp(C * lg_f)
        kcols = slice(h * RET_DK, (h + 1) * RET_DK)
        vcols = slice(h * RET_DV, (h + 1) * RET_DV)
        state = s_scr[h]
        for sub in range(n_sub):
            rows = slice(sub * C, (sub + 1) * C)
            q = q_ref[rows, kcols]
            k = k_ref[rows, kcols]
            v = v_ref[rows, vcols]
            scores = lax.dot_general(q, k, (((1,), (1,)), ((), ())), preferred_element_type=F32) * dmask
            o = (jnp.dot(scores.astype(BF16), v, preferred_element_type=F32)
                 + jnp.dot(q, state.astype(BF16), preferred_element_type=F32) * xi_f
                 + jnp.dot(q, t_ref[sub, h].astype(BF16), preferred_element_type=F32) * xi_b)
            kw = (k.astype(F32) * wk_f).astype(BF16)
            state = decay_f * state + lax.dot_general(kw, v, (((0,), (0,)), ((), ())), preferred_element_type=F32)
            mu = jnp.mean(o, axis=-1, keepdims=True)
            var = jnp.mean(jnp.square(o - mu), axis=-1, keepdims=True)
            o = (o - mu) * lax.rsqrt(var + EPS) * gn_ref[:, vcols]
            o_ref[rows, vcols] = (_silu(g_ref[rows, vcols]) * o).astype(o_ref.dtype)
        s_scr[h] = state
        st_ref[0, h] = state


def _retention(decay, rq, rk, rv, s0_f, s0_b, gate, gn, *, nb, nc, o_prev, o_blk0):
    R = RET_STEP_ROWS
    n_sub = R // RET_CHUNK
    state_spec = pl.BlockSpec((1, H_RET, RET_DK, RET_DV), lambda b, j: (b, 0, 0, 0))
    state_shape = jax.ShapeDtypeStruct((nb, H_RET, RET_DK, RET_DV), F32)
    smem = pl.BlockSpec(memory_space=pltpu.SMEM)
    rev = lambda b, j: (b * nc + (nc - 1 - j), 0)
    fwd = lambda b, j: (b * nc + j, 0)
    t_enter, s_b = pl.pallas_call(
        _ret_bwd_state_kernel,
        grid=(nb, nc),
        in_specs=[smem, pl.BlockSpec((R, H_RET * RET_DK), rev), pl.BlockSpec((R, H_RET * RET_DV), rev), state_spec],
        out_specs=(pl.BlockSpec((n_sub, H_RET, RET_DK, RET_DV), lambda b, j: (b * nc + (nc - 1 - j), 0, 0, 0)),
                   state_spec),
        out_shape=(jax.ShapeDtypeStruct((nb * nc * n_sub, H_RET, RET_DK, RET_DV), F32), state_shape),
        scratch_shapes=[pltpu.VMEM((H_RET, RET_DK, RET_DV), F32)],
        compiler_params=_cparams(("arbitrary", "arbitrary")),
        name="retention_bwd_states",
    )(decay, rk, rv, s0_b)

    in_specs = [smem, pl.BlockSpec((R, H_RET * RET_DK), fwd), pl.BlockSpec((R, H_RET * RET_DK), fwd),
                pl.BlockSpec((R, H_RET * RET_DV), fwd), state_spec,
                pl.BlockSpec((n_sub, H_RET, RET_DK, RET_DV), lambda b, j: (b * nc + j, 0, 0, 0)),
                pl.BlockSpec((R, H_RET * RET_DV), fwd), _full(gn.shape)]
    args = [decay, rq, rk, rv, s0_f, t_enter, gate, gn]
    aliases = {}
    if o_prev is not None:
        in_specs.append(pl.BlockSpec(memory_space=pl.ANY))
        args.append(o_prev)
        aliases = {len(args) - 1: 0}
    o_ret, s_f = pl.pallas_call(
        _ret_main_kernel,
        grid=(nb, nc),
        in_specs=in_specs,
        out_specs=(pl.BlockSpec((R, H_RET * RET_DV), lambda b, j: (o_blk0 + b * nc + j, 0)), state_spec),
        out_shape=(jax.ShapeDtypeStruct((NT, H_RET * RET_DV), BF16), state_shape),
        scratch_shapes=[pltpu.VMEM((H_RET, RET_DK, RET_DV), F32)],
        input_output_aliases=aliases,
        compiler_params=_cparams(("arbitrary", "arbitrary")),
        name="retention_main",
    )(*args)
    return o_ret, s_f, s_b


def _row_copy(src_ref, s, dst_ref, d, sem):
    return pltpu.make_async_copy(src_ref.at[pl.ds(s, 1), :], dst_ref.at[pl.ds(d, 1), :], sem)


def _wait_rows(src_ref, dst_ref, sem, n):
    def body(r, c):
        _row_copy(src_ref, 0, dst_ref, 0, sem).wait()
        return c
    lax.fori_loop(0, n, body, 0, unroll=ISSUE_UNROLL)


def _residual_and_route(x, y, mod_ref, gffn_ref, wrh_ref, wrl_ref, br_ref, x1_ref, route_ref, cnt_ref, xbuf_ref,
                        cnt_scr, stage_scr, slot_vm, slot_sm, row_sem, slot_sem, late_issue=False):
    i = pl.program_id(0)
    n_steps = pl.num_programs(0)

    @pl.when(i == 0)
    def _():
        cnt_scr[...] = jnp.zeros_like(cnt_scr)

    x1 = x + mod_ref[0, 2:3, :] * y
    x1_ref[...] = x1
    h2 = _prenorm(x1, gffn_ref[...], mod_ref[0, 3:4, :], mod_ref[0, 4:5, :])
    hi = h2.astype(BF16)
    hi32 = hi.astype(F32)
    lo = (h2 - hi32).astype(BF16)
    wrh = wrh_ref[...]
    lg = (jnp.dot(hi, wrh, preferred_element_type=F32) + jnp.dot(lo, wrh, preferred_element_type=F32)
          + jnp.dot(hi, wrl_ref[...], preferred_element_type=F32) + br_ref[...])
    lane = lax.broadcasted_iota(jnp.int32, lg.shape, 1).astype(F32)
    big = float(4 * LANES)
    gl = jnp.where((lane >= N_EXPERTS) & (lane < N_EXPERTS + N_GROUPS), lg, NEG)
    gmax = jnp.max(gl, axis=-1, keepdims=True)
    g_w = 1.0 / jnp.sum(jnp.exp(gl - gmax), axis=-1, keepdims=True)
    g_lane = jnp.min(jnp.where(gl == gmax, lane, big), axis=-1, keepdims=True)
    e_lo = (g_lane - N_EXPERTS) * EXP_PER_GROUP
    el = jnp.where((lane >= e_lo) & (lane < e_lo + EXP_PER_GROUP), lg, NEG)
    m1 = jnp.max(el, axis=-1, keepdims=True)
    i1 = jnp.min(jnp.where(el == m1, lane, big), axis=-1, keepdims=True)
    el2 = jnp.where(lane == i1, NEG, el)
    m2 = jnp.max(el2, axis=-1, keepdims=True)
    i2 = jnp.min(jnp.where(el2 == m2, lane, big), axis=-1, keepdims=True)
    t = jnp.exp(m2 - m1)
    w1 = g_w / (1.0 + t)
    w2 = w1 * t
    oh1 = lane == i1
    oh2 = lane == i2
    oh = jnp.where(oh1 | oh2, 1.0, 0.0)
    rr = lax.broadcasted_iota(jnp.int32, (TM, TM), 0)
    cc = lax.broadcasted_iota(jnp.int32, (TM, TM), 1)
    strict_lower = jnp.where(cc < rr, 1.0, 0.0).astype(BF16)
    before = jnp.dot(strict_lower, oh.astype(BF16), preferred_element_type=F32) + cnt_scr[...]
    slot1 = jnp.sum(jnp.where(oh1, before, 0.0), axis=-1, keepdims=True) + i1 * CAP
    slot2 = jnp.sum(jnp.where(oh2, before, 0.0), axis=-1, keepdims=True) + i2 * CAP
    cnt = cnt_scr[...] + jnp.sum(oh, axis=0, keepdims=True)
    cnt_scr[...] = cnt
    cnt_ref[...] = cnt
    route = jnp.where(lane == 0.0, i1, jnp.where(lane == 1.0, i2, jnp.where(lane == 2.0, w1, jnp.where(
        lane == 3.0, w2, jnp.where(lane == 4.0, slot1, jnp.where(lane == 5.0, slot2, 0.0))))))
    route_ref[...] = route

    disp = (xbuf_ref, stage_scr, slot_vm, slot_sm, row_sem, slot_sem)
    if late_issue:
        _wait_rows(stage_scr.at[0], xbuf_ref, row_sem, 2 * TM)
        _stage_tile(hi32, route, *disp[1:4], slot_sem)

        @pl.when(i == n_steps - 1)
        def _():
            _issue_all_rows(*disp[:4], row_sem)
            _wait_rows(stage_scr.at[0], xbuf_ref, row_sem, 2 * TM)
    else:
        @pl.when(i > 0)
        def _():
            _wait_rows(stage_scr.at[0], xbuf_ref, row_sem, 2 * TM)

        _stage_tile(hi32, route, *disp[1:4], slot_sem)
        _issue_all_rows(*disp[:4], row_sem)

        @pl.when(i == n_steps - 1)
        def _():
            _wait_rows(stage_scr.at[0], xbuf_ref, row_sem, 2 * TM)


def _stage_tile(hi32, route, stage_scr, slot_vm, slot_sm, slot_sem):
    word = (lax.bitcast_convert_type(hi32[:, :PACKED], U32)
            | (lax.bitcast_convert_type(hi32[:, PACKED:], U32) >> 16))
    stage_scr[...] = word.reshape(stage_scr.shape)
    slot_vm[...] = jnp.transpose(route)[0:SUBLANES, :].astype(jnp.int32)
    _slots_to_smem(slot_vm, slot_sm, slot_sem)


def _slots_to_smem(slot_vm, slot_sm, slot_sem):
    to_smem = pltpu.make_async_copy(slot_vm, slot_sm, slot_sem)
    to_smem.start()
    to_smem.wait()


def _issue_row(xbuf_ref, stage_scr, slot_sm, row_sem, g, u):
    src = stage_scr.at[g, pl.ds(u, 1), :]
    r = g * SUBLANES + u
    pltpu.make_async_copy(src, xbuf_ref.at[pl.ds(slot_sm[4, r], 1), :], row_sem).start()
    pltpu.make_async_copy(src, xbuf_ref.at[pl.ds(slot_sm[5, r], 1), :], row_sem).start()


def _issue_all_rows(xbuf_ref, stage_scr, slot_vm, slot_sm, row_sem):
    del slot_vm
    for r in range(TM):
        _issue_row(xbuf_ref, stage_scr, slot_sm, row_sem, r // SUBLANES, r % SUBLANES)


def _prime_late_issue(xbuf_ref, stage_scr, slot_vm, slot_sm, slot_sem):
    stage_scr[...] = jnp.zeros(stage_scr.shape, U32)
    col = lax.broadcasted_iota(jnp.int32, slot_vm.shape, 1)
    row = lax.broadcasted_iota(jnp.int32, slot_vm.shape, 0)
    slot_vm[...] = N_EXPERTS * CAP + col + jnp.where(row == 5, TM, 0)
    _slots_to_smem(slot_vm, slot_sm, slot_sem)


def _tail_specs():
    row = lambda i: (i, 0)
    in_specs = [pl.BlockSpec((1, 6, D_MODEL), lambda i: (_cond_row(i), 0, 0)),
                _full((1, D_MODEL)), _full((D_MODEL, LANES)), _full((D_MODEL, LANES)), _full((1, LANES))]
    out_specs = (pl.BlockSpec((TM, D_MODEL), row), pl.BlockSpec((TM, LANES), row), _full((1, LANES)),
                 pl.BlockSpec(memory_space=pl.ANY))
    out_shape = (jax.ShapeDtypeStruct((NT, D_MODEL), F32), jax.ShapeDtypeStruct((NT, LANES), F32),
                 jax.ShapeDtypeStruct((1, LANES), F32),
                 jax.ShapeDtypeStruct((N_EXPERTS * CAP + 2 * TM, PACKED), U32))
    scratch = [pltpu.VMEM((1, LANES), F32), pltpu.VMEM((TM // SUBLANES, SUBLANES, PACKED), U32),
               pltpu.VMEM((SUBLANES, TM), jnp.int32), pltpu.SMEM((SUBLANES, TM), jnp.int32),
               pltpu.SemaphoreType.DMA(()), pltpu.SemaphoreType.DMA(())]
    return in_specs, out_specs, out_shape, scratch


def _cond_row(i):
    return jnp.where(i < NP_TILES, DEC_BATCH, (i - NP_TILES) // S_TILES_PER_SEQ)


def _ctx_blk(i):
    return jnp.minimum(i, NP_TILES - 1)


def _lat_blk(i):
    return jnp.maximum(i - NP_TILES, 0)


def _even_out_kernel(xp_ref, xs_ref, o_ref, r_ref, wout_ref, *tail):
    y = jnp.dot(r_ref[...], wout_ref[H_MLA * MLA_V:, :], preferred_element_type=F32)
    for p in range(N_PAIR):
        y = y + jnp.dot(o_ref[p], wout_ref[p * LANES:(p + 1) * LANES, :], preferred_element_type=F32)
    x = jnp.where(pl.program_id(0) < NP_TILES, xp_ref[...], xs_ref[...])
    _residual_and_route(x, y, *tail)


def _even_out(xp, xs, o_mla, o_ret, wout, mod, gffn, wrh, wrl, br):
    row = lambda i: (i, 0)
    tail_in, out_specs, out_shape, scratch = _tail_specs()
    return pl.pallas_call(
        _even_out_kernel,
        grid=(NT_TILES,),
        in_specs=[pl.BlockSpec((TM, D_MODEL), lambda i: (_ctx_blk(i), 0)),
                  pl.BlockSpec((TM, D_MODEL), lambda i: (_lat_blk(i), 0)),
                  pl.BlockSpec((N_PAIR, TM, LANES), lambda i: (0, i, 0)),
                  pl.BlockSpec((TM, H_RET * RET_DV), row),
                  _full(wout.shape)] + tail_in,
        out_specs=out_specs,
        out_shape=out_shape,
        scratch_shapes=scratch,
        compiler_params=_cparams(("arbitrary",)),
        name="even_out_route",
    )(xp, xs, o_mla, o_ret, wout, mod, gffn, wrh, wrl, br)


def _conf_in_kernel(x_ref, mod_ref, g_ref, win_ref, u_ref):
    h = _prenorm(x_ref[...], g_ref[...], mod_ref[0, 0:1, :], mod_ref[0, 1:2, :])
    z = jnp.dot(h.astype(BF16), win_ref[...], preferred_element_type=F32)
    u_ref[...] = z[:, :CONV_DIM] * jax.nn.sigmoid(z[:, CONV_DIM:])


def _conf_in(x, mod, gain, win):
    row = lambda i: (i, 0)
    return pl.pallas_call(
        _conf_in_kernel,
        grid=(NT_TILES,),
        in_specs=[pl.BlockSpec((TM, D_MODEL), row),
                  pl.BlockSpec((1, 6, D_MODEL), lambda i: (_cond_row(i), 0, 0)),
                  _full(gain.shape), _full(win.shape)],
        out_specs=pl.BlockSpec((TM, CONV_DIM), row),
        out_shape=jax.ShapeDtypeStruct((NT, CONV_DIM), F32),
        compiler_params=_cparams(("arbitrary",)),
        name="conformer_in_glu",
    )(x, mod, gain, win)


CONV_ROWS = 32
SHIFT_ROWS = TM + 2 * HALO - SUBLANES


def _conv_out_kernel(x_ref, u_ref, ul_ref, ur_ref, dw_ref, dwb_ref, lng_ref, lnb_ref, wout_ref, *tail_and_scratch):
    tail = tail_and_scratch[:-3]
    ext_scr, shift_scr, act_scr = tail_and_scratch[-3:]
    i = pl.program_id(0)
    t = (i - NP_TILES) % S_TILES_PER_SEQ
    first = (i < NP_TILES) | (t == 0)
    last = (i < NP_TILES) | (t == S_TILES_PER_SEQ - 1)
    ext_scr[0:HALO, :] = jnp.where(first, 0.0, ul_ref[...])
    ext_scr[HALO:HALO + TM, :] = u_ref[...]
    ext_scr[HALO + TM:, :] = jnp.where(last, 0.0, ur_ref[...])
    for s in range(SUBLANES):
        shift_scr[s] = ext_scr[pl.ds(s, SHIFT_ROWS), :]
    xbuf_ref, stage_scr, slot_vm, slot_sm, row_sem, slot_sem = tail[8], *tail[10:15]

    @pl.when(i == 0)
    def _():
        _prime_late_issue(xbuf_ref, stage_scr, slot_vm, slot_sm, slot_sem)

    off = HALO - CONV_W // 2
    rows_per_block = CONV_ROWS
    for rb in range(TM // CONV_ROWS):
        acc = jnp.zeros((CONV_ROWS, CONV_DIM), F32)
        for kk in range(CONV_W):
            s, a = (off + kk) % SUBLANES, (off + kk) // SUBLANES * SUBLANES
            acc = acc + shift_scr[s, pl.ds(rb * CONV_ROWS + a, CONV_ROWS), :] * dw_ref[kk:kk + 1, :]
            for r in range(rb * rows_per_block + kk * rows_per_block // CONV_W,
                           rb * rows_per_block + (kk + 1) * rows_per_block // CONV_W):
                _issue_row(xbuf_ref, stage_scr, slot_sm, row_sem, r // SUBLANES, r % SUBLANES)
        c = acc + dwb_ref[...]
        mu = jnp.mean(c, axis=-1, keepdims=True)
        var = jnp.mean(jnp.square(c - mu), axis=-1, keepdims=True)
        c = (c - mu) * lax.rsqrt(var + EPS) * lng_ref[...] + lnb_ref[...]
        act_scr[rb * CONV_ROWS:(rb + 1) * CONV_ROWS, :] = _silu(c).astype(BF16)
    y = jnp.dot(act_scr[...], wout_ref[...], preferred_element_type=F32)
    _residual_and_route(x_ref[...], y, *tail, late_issue=True)


def _conv_out(x, u, dw, dwb, lng, lnb, wout, mod, gffn, wrh, wrl, br):
    row = lambda i: (i, 0)
    per = TM // HALO
    n_halo = NT // HALO
    tail_in, out_specs, out_shape, scratch = _tail_specs()
    return pl.pallas_call(
        _conv_out_kernel,
        grid=(NT_TILES,),
        in_specs=[pl.BlockSpec((TM, D_MODEL), row),
                  pl.BlockSpec((TM, CONV_DIM), row),
                  pl.BlockSpec((HALO, CONV_DIM), lambda i: (jnp.maximum(i * per - 1, 0), 0)),
                  pl.BlockSpec((HALO, CONV_DIM), lambda i: (jnp.minimum((i + 1) * per, n_halo - 1), 0)),
                  _full(dw.shape), _full(dwb.shape), _full(lng.shape), _full(lnb.shape), _full(wout.shape)] + tail_in,
        out_specs=out_specs,
        out_shape=out_shape,
        scratch_shapes=scratch + [pltpu.VMEM((TM + 2 * HALO, CONV_DIM), F32),
                                  pltpu.VMEM((SUBLANES, SHIFT_ROWS, CONV_DIM), F32),
                                  pltpu.VMEM((TM, CONV_DIM), BF16)],
        compiler_params=_cparams(("arbitrary",)),
        name="conformer_conv_out_route",
    )(x, u, u, u, dw, dwb, lng, lnb, wout, mod, gffn, wrh, wrl, br)


def _pad_fill_kernel(cnt_ref, xin_ref, xbuf_ref, zero_scr, sem):
    del xin_ref
    e = pl.program_id(0)
    n = cnt_ref[e]
    pad = (EB - n % EB) % EB
    zero_scr[...] = jnp.zeros_like(zero_scr)

    def start(r, c):
        _row_copy(zero_scr, 0, xbuf_ref, e * CAP + n + r, sem).start()
        return c

    def wait(r, c):
        _row_copy(zero_scr, 0, xbuf_ref, e * CAP + n + r, sem).wait()
        return c

    lax.fori_loop(0, pad, start, 0)
    lax.fori_loop(0, pad, wait, 0)


def _pad_fill(counts, xbuf):
    return pl.pallas_call(
        _pad_fill_kernel,
        grid_spec=pltpu.PrefetchScalarGridSpec(
            num_scalar_prefetch=1,
            grid=(N_EXPERTS,),
            in_specs=[pl.BlockSpec(memory_space=pl.ANY)],
            out_specs=pl.BlockSpec(memory_space=pl.ANY),
            scratch_shapes=[pltpu.VMEM((SUBLANES, PACKED), U32), pltpu.SemaphoreType.DMA(())]),
        out_shape=jax.ShapeDtypeStruct(xbuf.shape, xbuf.dtype),
        input_output_aliases={1: 0},
        compiler_params=_cparams(("arbitrary",)),
        name="moe_pad_fill",
    )(counts, xbuf)


def _expert_kernel(blk_ref, exp_ref, flag_ref, x_ref, wu_ref, wd_ref, o_ref, wu_scr, wd_scr):
    i = pl.program_id(0)
    flag = flag_ref[i]

    @pl.when((flag & 2) != 0)
    def _():
        wu_scr[...] = wu_ref[0, 0].astype(BF16)
        wd_scr[...] = wd_ref[0, 0].astype(BF16)

    @pl.when((flag & 1) != 0)
    def _():
        w = x_ref[...]
        xa = lax.bitcast_convert_type(w & jnp.uint32(0xFFFF0000), F32).astype(BF16)
        xb = lax.bitcast_convert_type(w << 16, F32).astype(BF16)
        ab = (jnp.dot(xa, wu_scr[:PACKED, :], preferred_element_type=F32)
              + jnp.dot(xb, wu_scr[PACKED:, :], preferred_element_type=F32))
        mid = (_silu(ab[:, :D_EXPERT]) * ab[:, D_EXPERT:]).astype(BF16)
        o_ref[...] = jnp.dot(mid, wd_scr[...], preferred_element_type=F32)


def _experts(blk, exp, flag, xbuf, w_up, w_down, layer):
    return pl.pallas_call(
        _expert_kernel,
        grid_spec=pltpu.PrefetchScalarGridSpec(
            num_scalar_prefetch=3,
            grid=(N_ITEMS,),
            in_specs=[pl.BlockSpec((EB, PACKED), lambda i, b, e, f: (b[i], 0)),
                      pl.BlockSpec((1, 1, D_MODEL, 2 * D_EXPERT), lambda i, b, e, f: (layer, e[i], 0, 0)),
                      pl.BlockSpec((1, 1, D_EXPERT, D_MODEL), lambda i, b, e, f: (layer, e[i], 0, 0))],
            out_specs=pl.BlockSpec((EB, D_MODEL), lambda i, b, e, f: (b[i], 0)),
            scratch_shapes=[pltpu.VMEM((D_MODEL, 2 * D_EXPERT), BF16), pltpu.VMEM((D_EXPERT, D_MODEL), BF16)]),
        out_shape=jax.ShapeDtypeStruct((N_EXPERTS * CAP, D_MODEL), F32),
        compiler_params=_cparams(("arbitrary",)),
        name="moe_experts",
    )(blk, exp, flag, xbuf, w_up, w_down)


def _combine_kernel(slot_ref, x_ref, route_ref, mod_ref, ybuf_ref, *rest, conformer_next):
    g_scr, sem = rest[-2:]
    if conformer_next:
        modn_ref, gain_ref, win_ref = rest[:3]
        outs = rest[3:-2]
    else:
        outs = rest[:-2]
    i = pl.program_id(0)
    n_steps = pl.num_programs(0)

    def gather(tile, buf):
        base = tile * (2 * TM)
        for r in range(TM):
            _row_copy(ybuf_ref, slot_ref[base + 2 * r], g_scr.at[buf], r, sem.at[buf]).start()
            _row_copy(ybuf_ref, slot_ref[base + 2 * r + 1], g_scr.at[buf], TM + r, sem.at[buf]).start()

    cur = i % 2

    @pl.when(i == 0)
    def _():
        gather(0, 0)

    for parity in range(2):
        @pl.when((i + 1 < n_steps) & (cur == parity))
        def _():
            gather(i + 1, 1 - parity)

    _wait_rows(ybuf_ref, g_scr.at[cur], sem.at[cur], 2 * TM)
    route = route_ref[...]
    y = route[:, 2:3] * g_scr[cur, 0:TM, :] + route[:, 3:4] * g_scr[cur, TM:2 * TM, :]
    res = x_ref[...] + mod_ref[0, 5:6, :] * y
    if conformer_next:
        outs[0][...] = res
        h = _prenorm(res, gain_ref[...], modn_ref[0, 0:1, :], modn_ref[0, 1:2, :])
        z = jnp.dot(h.astype(BF16), win_ref[...], preferred_element_type=F32)
        outs[1][...] = z[:, :CONV_DIM] * jax.nn.sigmoid(z[:, CONV_DIM:])
    elif len(outs) == 1:
        outs[0][...] = res
    else:
        @pl.when(i < NP_TILES)
        def _():
            outs[0][...] = res

        @pl.when(i >= NP_TILES)
        def _():
            outs[1][...] = res


def _combine(slots, x1, route, mod, ybuf, *, split, conformer_next=None):
    row = lambda i, s: (i, 0)
    extra_in, extra_args = [], []
    if conformer_next is not None:
        mod_n, gain_n, win_n = conformer_next
        extra_in = [pl.BlockSpec((1, 6, D_MODEL), lambda i, s: (_cond_row(i), 0, 0)), _full(gain_n.shape),
                    _full(win_n.shape)]
        extra_args = [mod_n, gain_n, win_n]
        out_specs = (pl.BlockSpec((TM, D_MODEL), row), pl.BlockSpec((TM, CONV_DIM), row))
        out_shape = (jax.ShapeDtypeStruct((NT, D_MODEL), F32), jax.ShapeDtypeStruct((NT, CONV_DIM), F32))
    elif split:
        out_specs = (pl.BlockSpec((TM, D_MODEL), lambda i, s: (_ctx_blk(i), 0)),
                     pl.BlockSpec((TM, D_MODEL), lambda i, s: (_lat_blk(i), 0)))
        out_shape = (jax.ShapeDtypeStruct((NP, D_MODEL), F32), jax.ShapeDtypeStruct((NS, D_MODEL), F32))
    else:
        out_specs = pl.BlockSpec((TM, D_MODEL), row)
        out_shape = jax.ShapeDtypeStruct((NT, D_MODEL), F32)
    return pl.pallas_call(
        functools.partial(_combine_kernel, conformer_next=conformer_next is not None),
        grid_spec=pltpu.PrefetchScalarGridSpec(
            num_scalar_prefetch=1,
            grid=(NT_TILES,),
            in_specs=[pl.BlockSpec((TM, D_MODEL), row),
                      pl.BlockSpec((TM, LANES), row),
                      pl.BlockSpec((1, 6, D_MODEL), lambda i, s: (_cond_row(i), 0, 0)),
                      pl.BlockSpec(memory_space=pl.ANY)] + extra_in,
            out_specs=out_specs,
            scratch_shapes=[pltpu.VMEM((2, 2 * TM, D_MODEL), F32), pltpu.SemaphoreType.DMA((2,))]),
        out_shape=out_shape,
        compiler_params=_cparams(("arbitrary",)),
        name="moe_combine",
    )(slots, x1, route, mod, ybuf, *extra_args)


def _moe(x1, xbuf, route, counts, mod, w_up, w_down, layer, *, split, conformer_next=None):
    slots = route[:, 4:6].astype(jnp.int32).reshape(-1)
    cnt = counts[0, :N_EXPERTS].astype(jnp.int32)
    nblk = (cnt + EB - 1) // EB
    ends = jnp.cumsum(nblk)
    total = ends[-1]
    item = jnp.arange(N_ITEMS, dtype=jnp.int32)
    valid = item < total
    item_c = jnp.minimum(item, total - 1)
    exp = jnp.minimum(jnp.sum((item_c[:, None] >= ends[None, :]).astype(jnp.int32), axis=1), N_EXPERTS - 1)
    j = item_c - (ends[exp] - nblk[exp])
    blk = (exp * CAP_BLOCKS + j).astype(jnp.int32)
    flag = (valid.astype(jnp.int32) + 2 * (valid & (j == 0)).astype(jnp.int32)).astype(jnp.int32)

    xbuf = _pad_fill(cnt, xbuf)
    ybuf = _experts(blk, exp, flag, xbuf, w_up, w_down, layer)
    return _combine(slots, x1, route, mod, ybuf, split=split, conformer_next=conformer_next)


def _pad_heads(w, width, real):
    lead = w.shape[:-1]
    w = w.reshape(lead + (H_MLA, real))
    w = jnp.pad(w, [(0, 0)] * len(lead) + [(0, 0), (0, width - real)])
    return w.reshape(lead + (H_MLA * width,))


def _rope_tables():
    L = DEC_SEQ
    rows = L // GRID_W
    r = jnp.repeat(jnp.arange(rows, dtype=F32), GRID_W)
    col = jnp.tile(jnp.arange(GRID_W, dtype=F32), rows)
    n_f = ROPE_DIM // 4
    freqs = ROPE_BASE ** (-jnp.arange(n_f, dtype=F32) / n_f)
    ang = jnp.concatenate([r[:, None] * freqs, col[:, None] * freqs], axis=-1)
    cos, sin = jnp.cos(ang), jnp.sin(ang)
    half = ROPE_DIM // 2
    z = lambda n: jnp.zeros((L, n), F32)
    o = lambda n: jnp.ones((L, n), F32)
    cos_t = jnp.concatenate([o(MLA_NOPE), cos, cos, o(LANES - MLA_QK)], axis=-1)
    sin_lo = jnp.concatenate([z(MLA_NOPE), -sin, z(half), z(LANES - MLA_QK)], axis=-1)
    sin_hi = jnp.concatenate([z(MLA_NOPE), z(half), sin, z(LANES - MLA_QK)], axis=-1)
    ident = (jnp.ones((TM, LANES), F32), jnp.zeros((TM, LANES), F32), jnp.zeros((TM, LANES), F32))
    return (cos_t, sin_lo, sin_hi), ident


def _even_weights(w_in, q_a_norm, w_q_up, kv_a_norm, w_kv_up, q_norm, k_norm):
    kr_cols = jnp.pad(w_in[:, Q_RANK + KV_RANK:Q_RANK + KV_RANK + ROPE_DIM], ((0, 0), (MLA_NOPE, LANES - MLA_QK)))
    rest = w_in[:, Q_RANK + KV_RANK + ROPE_DIM:]
    win = jnp.concatenate([w_in[:, :Q_RANK + KV_RANK], kr_cols, rest], axis=-1).astype(BF16)
    wq = _pad_heads(w_q_up, LANES, MLA_QK).astype(BF16)
    kv = w_kv_up.reshape(KV_RANK, H_MLA, MLA_NOPE + MLA_V)
    wk = _pad_heads(kv[:, :, :MLA_NOPE].reshape(KV_RANK, H_MLA * MLA_NOPE), LANES, MLA_NOPE).astype(BF16)
    wv = kv[:, :, MLA_NOPE:].reshape(KV_RANK, H_MLA * MLA_V).astype(BF16)
    padg = lambda g: jnp.pad(g, (0, LANES - MLA_QK)).reshape(1, LANES)
    qg = padg(q_norm) * (MLA_QK ** -0.5)
    kg = padg(k_norm)
    return (win, q_a_norm.reshape(1, -1), wq, kv_a_norm.reshape(1, -1), wk, wv, qg, kg)


def _router_weights(w_group, b_group, w_expert, b_expert):
    fill = LANES - N_EXPERTS - N_GROUPS
    w = jnp.pad(jnp.concatenate([w_expert, w_group], axis=1), ((0, 0), (0, fill)))
    b = jnp.pad(jnp.concatenate([b_expert, b_group]), (0, fill)).reshape(1, LANES)
    hi = w.astype(BF16)
    lo = (w - hi.astype(F32)).astype(BF16)
    return hi, lo, b


def kernel(x_prompt, x_sample, cache_mla_ckv, cache_mla_krope, state_retention, c, c_ctx, ada_w, ada_b, norm_mix, norm_ffn, ev_w_in, ev_q_a_norm, ev_w_q_up, ev_kv_a_norm, ev_w_kv_up, ev_q_norm, ev_k_norm, ev_ret_decay, ev_ret_gn, ev_w_out, od_w_in, od_dw, od_dw_b, od_ln_g, od_ln_b, od_w_out, moe_w_group, moe_b_group, moe_w_expert, moe_b_expert, moe_w_up, moe_w_down):
    depth = ada_w.shape[0]
    x, xp, xs = None, x_prompt.reshape(NP, D_MODEL), x_sample.reshape(NS, D_MODEL)
    cond = jnp.concatenate([c, c_ctx[None, :], jnp.zeros((2 * SUBLANES - DEC_BATCH - 1, D_MODEL), F32)], axis=0)
    mods = _ada(cond, ada_w, ada_b).reshape(depth, cond.shape[0], 6, D_MODEL)
    rope, rope_ident = _rope_tables()
    new_ckv, new_krope, new_state = [], [], []
    u = None

    for l in range(depth):
        mod = mods[l]
        jj = l // 2
        router = _router_weights(moe_w_group[l], moe_b_group[l], moe_w_expert[l], moe_b_expert[l])
        gmix = norm_mix[l].reshape(1, D_MODEL)
        gffn = norm_ffn[l].reshape(1, D_MODEL)
        if l % 2 == 0:
            wts = _even_weights(ev_w_in[jj], ev_q_a_norm[jj], ev_w_q_up[jj], ev_kv_a_norm[jj], ev_w_kv_up[jj],
                                ev_q_norm[jj], ev_k_norm[jj])
            if xp is None:
                xp, xs = x[:NP], x[NP:]
            qp, kp, vp, ckv_p, kr_p, rq_p, rk_p, rv_p, rg_p = _even_in(
                xp, mod, gmix, wts, rope_ident, n_tiles=NP_TILES, x_blk=lambda i: i, mod_row=lambda i: DEC_BATCH,
                rope_blk=lambda i: 0, kv_rows=NP, kv_blk=lambda i: i)
            qs, ks, vs, _, _, rq_s, rk_s, rv_s, rg_s = _even_in(
                xs, mod, gmix, wts, rope, n_tiles=NS // TM, x_blk=lambda i: i,
                mod_row=lambda i: i // S_TILES_PER_SEQ, rope_blk=lambda i: i % S_TILES_PER_SEQ,
                kv_rows=DEC_BATCH * LK_S,
                kv_blk=lambda i: (i // S_TILES_PER_SEQ) * KV_BLOCKS_PER_SEQ + 1 + i % S_TILES_PER_SEQ)
            kr_cache = jnp.pad(cache_mla_krope[:, jj].reshape(DEC_BATCH * PAST_LEN, ROPE_DIM),
                               ((0, 0), (MLA_NOPE, LANES - MLA_QK)))
            ks, vs = _ctx_kv(cache_mla_ckv[:, jj].reshape(DEC_BATCH * PAST_LEN, KV_RANK), kr_cache,
                             wts[4], wts[5], wts[7], rope_ident, ks, vs)
            o_mla = _attention(qp, kp, vp, None, nb=BATCH, tq=SEQ, lk=SEQ, o_row0=0, pairs=N_PAIR)
            o_mla = _attention(qs, ks, vs, o_mla, nb=DEC_BATCH, tq=TQ_LATENT, lk=LK_S, o_row0=NP, pairs=N_PAIR)

            decay = ev_ret_decay[jj]
            gn = ev_ret_gn[jj].reshape(1, -1)
            zero_state = jnp.zeros((BATCH, H_RET, RET_DK, RET_DV), F32)
            ncp, ncs = SEQ // RET_STEP_ROWS, DEC_SEQ // RET_STEP_ROWS
            o_ret, sf_p, sb_p = _retention(decay, rq_p, rk_p, rv_p, zero_state, zero_state, rg_p, gn,
                                           nb=BATCH, nc=ncp, o_prev=None, o_blk0=0)
            o_ret, _, _ = _retention(decay, rq_s, rk_s, rv_s, state_retention[:, jj, 0], state_retention[:, jj, 1],
                                     rg_s, gn, nb=DEC_BATCH, nc=ncs, o_prev=o_ret, o_blk0=NP // RET_STEP_ROWS)
            new_ckv.append(ckv_p.reshape(BATCH, SEQ, KV_RANK))
            new_krope.append(kr_p[:, MLA_NOPE:MLA_QK].reshape(BATCH, SEQ, ROPE_DIM))
            new_state.append(jnp.stack([sf_p, sb_p], axis=1))
            x1, route, counts, xbuf = _even_out(xp, xs, o_mla, o_ret, ev_w_out[jj].astype(BF16), mod, gffn, *router)
        else:
            x = jnp.concatenate([xp, xs], axis=0) if x is None else x
            if u is None:
                u = _conf_in(x, mod, gmix, od_w_in[jj].astype(BF16))
            x1, route, counts, xbuf = _conv_out(
                x, u, od_dw[jj], od_dw_b[jj].reshape(1, -1), od_ln_g[jj].reshape(1, -1), od_ln_b[jj].reshape(1, -1),
                od_w_out[jj].astype(BF16), mod, gffn, *router)
        u = None
        if l == depth - 1:
            xp, xs = _moe(x1, xbuf, route, counts, mod, moe_w_up, moe_w_down, l, split=True)
            x = None
        elif (l + 1) % 2 == 1:
            nxt = (mods[l + 1], norm_mix[l + 1].reshape(1, D_MODEL), od_w_in[(l + 1) // 2].astype(BF16))
            x, u = _moe(x1, xbuf, route, counts, mod, moe_w_up, moe_w_down, l, split=False, conformer_next=nxt)
            xp, xs = None, None
        else:
            x = _moe(x1, xbuf, route, counts, mod, moe_w_up, moe_w_down, l, split=False)
            xp, xs = None, None

    return (xp.reshape(BATCH, SEQ, D_MODEL), xs.reshape(DEC_BATCH, DEC_SEQ, D_MODEL),
            jnp.stack(new_ckv, axis=1), jnp.stack(new_krope, axis=1), jnp.stack(new_state, axis=1))
```

```python
import functools

import jax
import jax.numpy as jnp
from jax import lax
from jax.experimental import pallas as pl
from jax.experimental.pallas import tpu as pltpu

F32 = jnp.float32
BF16 = jnp.bfloat16
U32 = jnp.uint32

D_MODEL = 1024
BATCH, SEQ = 32, 256
DEC_BATCH, DEC_SEQ = 8, 4096
PAST_LEN = 256
GRID_W = 64
EPS = 1e-6
H_MLA, Q_RANK, KV_RANK = 8, 256, 128
MLA_NOPE, ROPE_DIM, MLA_V = 64, 32, 64
MLA_QK = MLA_NOPE + ROPE_DIM
ROPE_BASE = 10000.0
H_RET, RET_DK, RET_DV, RET_CHUNK = 4, 64, 128, 128
CONV_DIM, CONV_W = 1024, 31
N_GROUPS, EXP_PER_GROUP, D_EXPERT = 4, 8, 256
N_EXPERTS = N_GROUPS * EXP_PER_GROUP

LANES = 128
SUBLANES = 8
VMEM_LIMIT = 48 * 1024 * 1024

NP = BATCH * SEQ
NS = DEC_BATCH * DEC_SEQ
NT = NP + NS
TM = 256
NT_TILES = NT // TM
NP_TILES = NP // TM
S_TILES_PER_SEQ = DEC_SEQ // TM
LK_S = PAST_LEN + DEC_SEQ
KV_BLOCKS_PER_SEQ = LK_S // TM
N_PAIR = H_MLA // 2
TQ_LATENT = 256
TQ_CHAIN = 256
RET_STEP_ROWS = 2 * RET_CHUNK
EB = 256
CAP = NT + EB
CAP_BLOCKS = CAP // EB
N_ITEMS = (2 * NT) // EB + N_EXPERTS
HALO = 16
PACKED = D_MODEL // 2
ISSUE_UNROLL = 8
NEG = -1e30


def _cparams(sem):
    return pltpu.CompilerParams(dimension_semantics=sem, vmem_limit_bytes=VMEM_LIMIT)


def _full(shape):
    n = len(shape)
    return pl.BlockSpec(shape, lambda *_: (0,) * n)


def _rms(x, gain):
    return x * lax.rsqrt(jnp.mean(x * x, axis=-1, keepdims=True) + EPS) * gain


def _prenorm(x, gain, shift, scale):
    return _rms(x, gain) * (1.0 + scale) + shift


def _silu(x):
    return x * jax.nn.sigmoid(x)


def _rope128(x, cos, sin_lo, sin_hi):
    return x * cos + pltpu.roll(x, LANES - ROPE_DIM // 2, 1) * sin_lo + pltpu.roll(x, ROPE_DIM // 2, 1) * sin_hi


def _head_norm_rope(slab, gain, cos, sin_lo, sin_hi):
    r = lax.rsqrt(jnp.sum(slab * slab, axis=-1, keepdims=True) * (1.0 / MLA_QK) + EPS)
    return _rope128(slab * r * gain, cos, sin_lo, sin_hi)


def _ada_kernel(c_ref, w_ref, b_ref, o_ref):
    s = _silu(c_ref[...]).astype(BF16)
    o_ref[0] = jnp.dot(s, w_ref[0].astype(BF16), preferred_element_type=F32) + b_ref[0]


def _ada(cond, ada_w, ada_b):
    depth, d, n = ada_w.shape
    rows = cond.shape[0]
    tn = 1536
    return pl.pallas_call(
        _ada_kernel,
        grid=(depth, n // tn),
        in_specs=[pl.BlockSpec((rows, d), lambda l, j: (0, 0)),
                  pl.BlockSpec((1, d, tn), lambda l, j: (l, 0, j)),
                  pl.BlockSpec((1, 1, tn), lambda l, j: (l, 0, j))],
        out_specs=pl.BlockSpec((1, rows, tn), lambda l, j: (l, 0, j)),
        out_shape=jax.ShapeDtypeStruct((depth, rows, n), F32),
        compiler_params=_cparams(("arbitrary", "arbitrary")),
        name="ada_modulation",
    )(cond, ada_w, ada_b.reshape(depth, 1, n))


def _kv_heads(ckvn_bf, kr_slab, wk_ref, wv_ref, kg, cos, sin_lo, sin_hi, k_ref, v_ref):
    kk = jnp.dot(ckvn_bf, wk_ref[...], preferred_element_type=F32)
    vv = jnp.dot(ckvn_bf, wv_ref[...], preferred_element_type=F32)
    for h in range(H_MLA):
        kh = kk[:, h * LANES:(h + 1) * LANES] + kr_slab
        k_ref[h] = _head_norm_rope(kh, kg, cos, sin_lo, sin_hi).astype(BF16)
    for j in range(N_PAIR):
        v_ref[j] = vv[:, j * LANES:(j + 1) * LANES].astype(BF16)


def _even_in_kernel(x_ref, mod_ref, g_ref, win_ref, qan_ref, wq_ref, kvan_ref, wk_ref, wv_ref, qg_ref, kg_ref,
                    cos_ref, sl_ref, sh_ref,
                    q_ref, k_ref, v_ref, ckv_ref, kr_ref, rq_ref, rk_ref, rv_ref, rg_ref):
    h = _prenorm(x_ref[...], g_ref[...], mod_ref[0, 0:1, :], mod_ref[0, 1:2, :])
    z = jnp.dot(h.astype(BF16), win_ref[...], preferred_element_type=F32)
    cq = z[:, 0:256]
    ckv = z[:, 256:384]
    kr_slab = z[:, 384:512]
    cos, sin_lo, sin_hi = cos_ref[...], sl_ref[...], sh_ref[...]

    qq = jnp.dot(_rms(cq, qan_ref[...]).astype(BF16), wq_ref[...], preferred_element_type=F32)
    qg = qg_ref[...]
    for hh in range(H_MLA):
        q_ref[hh] = _head_norm_rope(qq[:, hh * LANES:(hh + 1) * LANES], qg, cos, sin_lo, sin_hi).astype(BF16)

    ckvn = _rms(ckv, kvan_ref[...])
    ckv_ref[...] = ckvn
    kr_ref[...] = kr_slab
    _kv_heads(ckvn.astype(BF16), kr_slab, wk_ref, wv_ref, kg_ref[...], cos, sin_lo, sin_hi, k_ref, v_ref)

    rq_ref[...] = z[:, 512:768].astype(BF16)
    rk_ref[...] = (z[:, 768:1024] * (RET_DK ** -0.5)).astype(BF16)
    rv_ref[...] = z[:, 1024:1536].astype(BF16)
    rg_ref[...] = z[:, 1536:2048]


def _even_in(x, mod, gain, wts, rope, *, n_tiles, x_blk, mod_row, rope_blk, kv_rows, kv_blk):
    win, qan, wq, kvan, wk, wv, qg, kg = wts
    cos, sin_lo, sin_hi = rope
    ntok = n_tiles * TM
    row = lambda f: (lambda i: (f(i), 0))
    tab = pl.BlockSpec((TM, LANES), row(rope_blk))
    heads = lambda f: (lambda i: (0, f(i), 0))
    out_shape = (
        jax.ShapeDtypeStruct((H_MLA, ntok, LANES), BF16),
        jax.ShapeDtypeStruct((H_MLA, kv_rows, LANES), BF16),
        jax.ShapeDtypeStruct((N_PAIR, kv_rows, LANES), BF16),
        jax.ShapeDtypeStruct((ntok, KV_RANK), F32),
        jax.ShapeDtypeStruct((ntok, LANES), F32),
        jax.ShapeDtypeStruct((ntok, H_RET * RET_DK), BF16),
        jax.ShapeDtypeStruct((ntok, H_RET * RET_DK), BF16),
        jax.ShapeDtypeStruct((ntok, H_RET * RET_DV), BF16),
        jax.ShapeDtypeStruct((ntok, H_RET * RET_DV), F32),
    )
    ident = lambda i: i
    out_specs = (
        pl.BlockSpec((H_MLA, TM, LANES), heads(ident)),
        pl.BlockSpec((H_MLA, TM, LANES), heads(kv_blk)),
        pl.BlockSpec((N_PAIR, TM, LANES), heads(kv_blk)),
        pl.BlockSpec((TM, KV_RANK), row(ident)),
        pl.BlockSpec((TM, LANES), row(ident)),
        pl.BlockSpec((TM, H_RET * RET_DK), row(ident)),
        pl.BlockSpec((TM, H_RET * RET_DK), row(ident)),
        pl.BlockSpec((TM, H_RET * RET_DV), row(ident)),
        pl.BlockSpec((TM, H_RET * RET_DV), row(ident)),
    )
    return pl.pallas_call(
        _even_in_kernel,
        grid=(n_tiles,),
        in_specs=[pl.BlockSpec((TM, D_MODEL), row(x_blk)),
                  pl.BlockSpec((1, 6, D_MODEL), lambda i: (mod_row(i), 0, 0)),
                  _full(gain.shape), _full(win.shape), _full(qan.shape), _full(wq.shape), _full(kvan.shape),
                  _full(wk.shape), _full(wv.shape), _full(qg.shape), _full(kg.shape), tab, tab, tab],
        out_specs=out_specs,
        out_shape=out_shape,
        compiler_params=_cparams(("arbitrary",)),
        name="even_in_proj",
    )(x, mod, gain, win, qan, wq, kvan, wk, wv, qg, kg, cos, sin_lo, sin_hi)


def _ctx_kv_kernel(ckv_ref, kr_ref, wk_ref, wv_ref, kg_ref, cos_ref, sl_ref, sh_ref, k_in, v_in, k_ref, v_ref):
    del k_in, v_in
    _kv_heads(ckv_ref[...].astype(BF16), kr_ref[...], wk_ref, wv_ref, kg_ref[...],
              cos_ref[...], sl_ref[...], sh_ref[...], k_ref, v_ref)


def _ctx_kv(cache_ckv, cache_kr_slab, wk, wv, kg, rope_ident, k_all, v_all):
    cos, sin_lo, sin_hi = rope_ident
    nb = cache_ckv.shape[0] // PAST_LEN
    blk = lambda b: (0, b * KV_BLOCKS_PER_SEQ, 0)
    any_spec = pl.BlockSpec(memory_space=pl.ANY)
    return pl.pallas_call(
        _ctx_kv_kernel,
        grid=(nb,),
        in_specs=[pl.BlockSpec((PAST_LEN, KV_RANK), lambda b: (b, 0)),
                  pl.BlockSpec((PAST_LEN, LANES), lambda b: (b, 0)),
                  _full(wk.shape), _full(wv.shape), _full(kg.shape),
                  _full(cos.shape), _full(cos.shape), _full(cos.shape), any_spec, any_spec],
        out_specs=(pl.BlockSpec((H_MLA, PAST_LEN, LANES), blk), pl.BlockSpec((N_PAIR, PAST_LEN, LANES), blk)),
        out_shape=(jax.ShapeDtypeStruct(k_all.shape, k_all.dtype), jax.ShapeDtypeStruct(v_all.shape, v_all.dtype)),
        input_output_aliases={8: 0, 9: 1},
        compiler_params=_cparams(("arbitrary",)),
        name="ctx_kv_heads",
    )(cache_ckv, cache_kr_slab, wk, wv, kg, cos, sin_lo, sin_hi, k_all, v_all)


def _attn_kernel(q_ref, k_ref, v_ref, *rest):
    o_ref = rest[-1]
    chain_rows = min(TQ_CHAIN, q_ref.shape[1])
    for qt in range(q_ref.shape[1] // chain_rows):
        rows = slice(qt * chain_rows, (qt + 1) * chain_rows)
        for pr in range(v_ref.shape[0]):
            v = v_ref[pr]
            outs = []
            for a in range(2):
                h = 2 * pr + a
                s = lax.dot_general(q_ref[h, rows, :], k_ref[h], (((1,), (1,)), ((), ())),
                                    preferred_element_type=F32)
                p = jnp.exp(s - jnp.max(s, axis=-1, keepdims=True))
                l = jnp.sum(p, axis=-1, keepdims=True)
                outs.append(jnp.dot(p.astype(BF16), v, preferred_element_type=F32) / l)
            lane = lax.broadcasted_iota(jnp.int32, outs[0].shape, 1)
            o_ref[pr, rows, :] = jnp.where(lane < MLA_V, outs[0], outs[1]).astype(BF16)


def _attention(q, k, v, o_prev, *, nb, tq, lk, o_row0, pairs):
    nq = q.shape[1] // (nb * tq)
    o_blk0 = o_row0 // tq
    in_specs = [pl.BlockSpec((2 * pairs, tq, LANES), lambda b, p, i: (p, b * nq + i, 0)),
                pl.BlockSpec((2 * pairs, lk, LANES), lambda b, p, i: (p, b, 0)),
                pl.BlockSpec((pairs, lk, LANES), lambda b, p, i: (p, b, 0))]
    args = [q, k, v]
    aliases = {}
    if o_prev is not None:
        in_specs.append(pl.BlockSpec(memory_space=pl.ANY))
        args.append(o_prev)
        aliases = {3: 0}
    return pl.pallas_call(
        _attn_kernel,
        grid=(nb, N_PAIR // pairs, nq),
        in_specs=in_specs,
        out_specs=pl.BlockSpec((pairs, tq, LANES), lambda b, p, i: (p, o_blk0 + b * nq + i, 0)),
        out_shape=jax.ShapeDtypeStruct((N_PAIR, NT, LANES), BF16),
        input_output_aliases=aliases,
        compiler_params=_cparams(("arbitrary", "arbitrary", "arbitrary")),
        name="mla_attention",
    )(*args)


def _log_sigmoid_lanes(x):
    dl = jnp.full((1, LANES), x, F32)
    return jnp.minimum(dl, 0.0) - jnp.log1p(jnp.exp(-jnp.abs(dl)))


def _ret_bwd_state_kernel(dec_ref, k_ref, v_ref, s0_ref, t_ref, st_ref, s_scr):
    C = RET_CHUNK

    @pl.when(pl.program_id(1) == 0)
    def _():
        s_scr[...] = s0_ref[0]

    rowk = lax.broadcasted_iota(jnp.int32, (C, RET_DK), 0).astype(F32)
    n_sub = k_ref.shape[0] // C
    for h in range(H_RET):
        lg = _log_sigmoid_lanes(dec_ref[1, h])
        wk = jnp.exp(rowk * lg[:, :RET_DK])
        chunk_decay = jnp.exp(C * lg)
        state = s_scr[h]
        for sub in reversed(range(n_sub)):
            rows = slice(sub * C, (sub + 1) * C)
            t_ref[sub, h] = state
            kw = (k_ref[rows, h * RET_DK:(h + 1) * RET_DK].astype(F32) * wk).astype(BF16)
            u = lax.dot_general(kw, v_ref[rows, h * RET_DV:(h + 1) * RET_DV], (((0,), (0,)), ((), ())),
                                preferred_element_type=F32)
            state = chunk_decay * state + u
        s_scr[h] = state
        st_ref[0, h] = state


def _ret_main_kernel(dec_ref, q_ref, k_ref, v_ref, s0_ref, t_ref, g_ref, gn_ref, *rest):
    o_ref, st_ref, s_scr = rest[-3:]
    C = RET_CHUNK

    @pl.when(pl.program_id(1) == 0)
    def _():
        s_scr[...] = s0_ref[0]

    row = lax.broadcasted_iota(jnp.int32, (C, C), 0).astype(F32)
    col = lax.broadcasted_iota(jnp.int32, (C, C), 1).astype(F32)
    dist = row - col
    rowv = lax.broadcasted_iota(jnp.int32, (C, RET_DV), 0).astype(F32)
    rowk = lax.broadcasted_iota(jnp.int32, (C, RET_DK), 0).astype(F32)
    n_sub = q_ref.shape[0] // C
    for h in range(H_RET):
        lg_f = _log_sigmoid_lanes(dec_ref[0, h])
        lg_b = _log_sigmoid_lanes(dec_ref[1, h])
        dmask = (jnp.where(dist >= 0.0, jnp.exp(jnp.maximum(dist, 0.0) * lg_f), 0.0)
                 + jnp.where(dist <= 0.0, jnp.exp(jnp.maximum(-dist, 0.0) * lg_b), 0.0))
        xi_f = jnp.exp((rowv + 1.0) * lg_f)
        xi_b = jnp.exp((C - rowv) * lg_b)
        wk_f = jnp.exp((C - 1.0 - rowk) * lg_f[:, :RET_DK])
        decay_f = jnp.exp(C * lg_f)
        kcols = slice(h * RET_DK, (h + 1) * RET_DK)
        vcols = slice(h * RET_DV, (h + 1) * RET_DV)
        state = s_scr[h]
        for sub in range(n_sub):
            rows = slice(sub * C, (sub + 1) * C)
            q = q_ref[rows, kcols]
            k = k_ref[rows, kcols]
            v = v_ref[rows, vcols]
            scores = lax.dot_general(q, k, (((1,), (1,)), ((), ())), preferred_element_type=F32) * dmask
            o = (jnp.dot(scores.astype(BF16), v, preferred_element_type=F32)
                 + jnp.dot(q, state.astype(BF16), preferred_element_type=F32) * xi_f
                 + jnp.dot(q, t_ref[sub, h].astype(BF16), preferred_element_type=F32) * xi_b)
            kw = (k.astype(F32) * wk_f).astype(BF16)
            state = decay_f * state + lax.dot_general(kw, v, (((0,), (0,)), ((), ())), preferred_element_type=F32)
            mu = jnp.mean(o, axis=-1, keepdims=True)
            var = jnp.mean(jnp.square(o - mu), axis=-1, keepdims=True)
            o = (o - mu) * lax.rsqrt(var + EPS) * gn_ref[:, vcols]
            o_ref[rows, vcols] = (_silu(g_ref[rows, vcols]) * o).astype(o_ref.dtype)
        s_scr[h] = state
        st_ref[0, h] = state


def _retention(decay, rq, rk, rv, s0_f, s0_b, gate, gn, *, nb, nc, o_prev, o_blk0):
    R = RET_STEP_ROWS
    n_sub = R // RET_CHUNK
    state_spec = pl.BlockSpec((1, H_RET, RET_DK, RET_DV), lambda b, j: (b, 0, 0, 0))
    state_shape = jax.ShapeDtypeStruct((nb, H_RET, RET_DK, RET_DV), F32)
    smem = pl.BlockSpec(memory_space=pltpu.SMEM)
    rev = lambda b, j: (b * nc + (nc - 1 - j), 0)
    fwd = lambda b, j: (b * nc + j, 0)
    t_enter, s_b = pl.pallas_call(
        _ret_bwd_state_kernel,
        grid=(nb, nc),
        in_specs=[smem, pl.BlockSpec((R, H_RET * RET_DK), rev), pl.BlockSpec((R, H_RET * RET_DV), rev), state_spec],
        out_specs=(pl.BlockSpec((n_sub, H_RET, RET_DK, RET_DV), lambda b, j: (b * nc + (nc - 1 - j), 0, 0, 0)),
                   state_spec),
        out_shape=(jax.ShapeDtypeStruct((nb * nc * n_sub, H_RET, RET_DK, RET_DV), F32), state_shape),
        scratch_shapes=[pltpu.VMEM((H_RET, RET_DK, RET_DV), F32)],
        compiler_params=_cparams(("arbitrary", "arbitrary")),
        name="retention_bwd_states",
    )(decay, rk, rv, s0_b)

    in_specs = [smem, pl.BlockSpec((R, H_RET * RET_DK), fwd), pl.BlockSpec((R, H_RET * RET_DK), fwd),
                pl.BlockSpec((R, H_RET * RET_DV), fwd), state_spec,
                pl.BlockSpec((n_sub, H_RET, RET_DK, RET_DV), lambda b, j: (b * nc + j, 0, 0, 0)),
                pl.BlockSpec((R, H_RET * RET_DV), fwd), _full(gn.shape)]
    args = [decay, rq, rk, rv, s0_f, t_enter, gate, gn]
    aliases = {}
    if o_prev is not None:
        in_specs.append(pl.BlockSpec(memory_space=pl.ANY))
        args.append(o_prev)
        aliases = {len(args) - 1: 0}
    o_ret, s_f = pl.pallas_call(
        _ret_main_kernel,
        grid=(nb, nc),
        in_specs=in_specs,
        out_specs=(pl.BlockSpec((R, H_RET * RET_DV), lambda b, j: (o_blk0 + b * nc + j, 0)), state_spec),
        out_shape=(jax.ShapeDtypeStruct((NT, H_RET * RET_DV), BF16), state_shape),
        scratch_shapes=[pltpu.VMEM((H_RET, RET_DK, RET_DV), F32)],
        input_output_aliases=aliases,
        compiler_params=_cparams(("arbitrary", "arbitrary")),
        name="retention_main",
    )(*args)
    return o_ret, s_f, s_b


def _row_copy(src_ref, s, dst_ref, d, sem):
    return pltpu.make_async_copy(src_ref.at[pl.ds(s, 1), :], dst_ref.at[pl.ds(d, 1), :], sem)


def _wait_rows(src_ref, dst_ref, sem, n):
    def body(r, c):
        _row_copy(src_ref, 0, dst_ref, 0, sem).wait()
        return c
    lax.fori_loop(0, n, body, 0, unroll=ISSUE_UNROLL)


def _residual_and_route(x, y, mod_ref, gffn_ref, wrh_ref, wrl_ref, br_ref, x1_ref, route_ref, cnt_ref, xbuf_ref,
                        cnt_scr, stage_scr, slot_vm, slot_sm, row_sem, slot_sem, late_issue=False):
    i = pl.program_id(0)
    n_steps = pl.num_programs(0)

    @pl.when(i == 0)
    def _():
        cnt_scr[...] = jnp.zeros_like(cnt_scr)

    x1 = x + mod_ref[0, 2:3, :] * y
    x1_ref[...] = x1
    h2 = _prenorm(x1, gffn_ref[...], mod_ref[0, 3:4, :], mod_ref[0, 4:5, :])
    hi = h2.astype(BF16)
    hi32 = hi.astype(F32)
    lo = (h2 - hi32).astype(BF16)
    wrh = wrh_ref[...]
    lg = (jnp.dot(hi, wrh, preferred_element_type=F32) + jnp.dot(lo, wrh, preferred_element_type=F32)
          + jnp.dot(hi, wrl_ref[...], preferred_element_type=F32) + br_ref[...])
    lane = lax.broadcasted_iota(jnp.int32, lg.shape, 1).astype(F32)
    big = float(4 * LANES)
    gl = jnp.where((lane >= N_EXPERTS) & (lane < N_EXPERTS + N_GROUPS), lg, NEG)
    gmax = jnp.max(gl, axis=-1, keepdims=True)
    g_w = 1.0 / jnp.sum(jnp.exp(gl - gmax), axis=-1, keepdims=True)
    g_lane = jnp.min(jnp.where(gl == gmax, lane, big), axis=-1, keepdims=True)
    e_lo = (g_lane - N_EXPERTS) * EXP_PER_GROUP
    el = jnp.where((lane >= e_lo) & (lane < e_lo + EXP_PER_GROUP), lg, NEG)
    m1 = jnp.max(el, axis=-1, keepdims=True)
    i1 = jnp.min(jnp.where(el == m1, lane, big), axis=-1, keepdims=True)
    el2 = jnp.where(lane == i1, NEG, el)
    m2 = jnp.max(el2, axis=-1, keepdims=True)
    i2 = jnp.min(jnp.where(el2 == m2, lane, big), axis=-1, keepdims=True)
    t = jnp.exp(m2 - m1)
    w1 = g_w / (1.0 + t)
    w2 = w1 * t
    oh1 = lane == i1
    oh2 = lane == i2
    oh = jnp.where(oh1 | oh2, 1.0, 0.0)
    rr = lax.broadcasted_iota(jnp.int32, (TM, TM), 0)
    cc = lax.broadcasted_iota(jnp.int32, (TM, TM), 1)
    strict_lower = jnp.where(cc < rr, 1.0, 0.0).astype(BF16)
    before = jnp.dot(strict_lower, oh.astype(BF16), preferred_element_type=F32) + cnt_scr[...]
    slot1 = jnp.sum(jnp.where(oh1, before, 0.0), axis=-1, keepdims=True) + i1 * CAP
    slot2 = jnp.sum(jnp.where(oh2, before, 0.0), axis=-1, keepdims=True) + i2 * CAP
    cnt = cnt_scr[...] + jnp.sum(oh, axis=0, keepdims=True)
    cnt_scr[...] = cnt
    cnt_ref[...] = cnt
    route = jnp.where(lane == 0.0, i1, jnp.where(lane == 1.0, i2, jnp.where(lane == 2.0, w1, jnp.where(
        lane == 3.0, w2, jnp.where(lane == 4.0, slot1, jnp.where(lane == 5.0, slot2, 0.0))))))
    route_ref[...] = route

    disp = (xbuf_ref, stage_scr, slot_vm, slot_sm, row_sem, slot_sem)
    if late_issue:
        _wait_rows(stage_scr.at[0], xbuf_ref, row_sem, 2 * TM)
        _stage_tile(hi32, route, *disp[1:4], slot_sem)

        @pl.when(i == n_steps - 1)
        def _():
            _issue_all_rows(*disp[:4], row_sem)
            _wait_rows(stage_scr.at[0], xbuf_ref, row_sem, 2 * TM)
    else:
        @pl.when(i > 0)
        def _():
            _wait_rows(stage_scr.at[0], xbuf_ref, row_sem, 2 * TM)

        _stage_tile(hi32, route, *disp[1:4], slot_sem)
        _issue_all_rows(*disp[:4], row_sem)

        @pl.when(i == n_steps - 1)
        def _():
            _wait_rows(stage_scr.at[0], xbuf_ref, row_sem, 2 * TM)


def _stage_tile(hi32, route, stage_scr, slot_vm, slot_sm, slot_sem):
    word = (lax.bitcast_convert_type(hi32[:, :PACKED], U32)
            | (lax.bitcast_convert_type(hi32[:, PACKED:], U32) >> 16))
    stage_scr[...] = word.reshape(stage_scr.shape)
    slot_vm[...] = jnp.transpose(route)[0:SUBLANES, :].astype(jnp.int32)
    _slots_to_smem(slot_vm, slot_sm, slot_sem)


def _slots_to_smem(slot_vm, slot_sm, slot_sem):
    to_smem = pltpu.make_async_copy(slot_vm, slot_sm, slot_sem)
    to_smem.start()
    to_smem.wait()


def _issue_row(xbuf_ref, stage_scr, slot_sm, row_sem, g, u):
    src = stage_scr.at[g, pl.ds(u, 1), :]
    r = g * SUBLANES + u
    pltpu.make_async_copy(src, xbuf_ref.at[pl.ds(slot_sm[4, r], 1), :], row_sem).start(priority=0)
    pltpu.make_async_copy(src, xbuf_ref.at[pl.ds(slot_sm[5, r], 1), :], row_sem).start(priority=1)


def _issue_all_rows(xbuf_ref, stage_scr, slot_vm, slot_sm, row_sem):
    del slot_vm
    for r in range(TM):
        _issue_row(xbuf_ref, stage_scr, slot_sm, row_sem, r // SUBLANES, r % SUBLANES)


def _prime_late_issue(xbuf_ref, stage_scr, slot_vm, slot_sm, slot_sem):
    stage_scr[...] = jnp.zeros(stage_scr.shape, U32)
    col = lax.broadcasted_iota(jnp.int32, slot_vm.shape, 1)
    row = lax.broadcasted_iota(jnp.int32, slot_vm.shape, 0)
    slot_vm[...] = N_EXPERTS * CAP + col + jnp.where(row == 5, TM, 0)
    _slots_to_smem(slot_vm, slot_sm, slot_sem)


def _tail_specs():
    row = lambda i: (i, 0)
    in_specs = [pl.BlockSpec((1, 6, D_MODEL), lambda i: (_cond_row(i), 0, 0)),
                _full((1, D_MODEL)), _full((D_MODEL, LANES)), _full((D_MODEL, LANES)), _full((1, LANES))]
    out_specs = (pl.BlockSpec((TM, D_MODEL), row), pl.BlockSpec((TM, LANES), row), _full((1, LANES)),
                 pl.BlockSpec(memory_space=pl.ANY))
    out_shape = (jax.ShapeDtypeStruct((NT, D_MODEL), F32), jax.ShapeDtypeStruct((NT, LANES), F32),
                 jax.ShapeDtypeStruct((1, LANES), F32),
                 jax.ShapeDtypeStruct((N_EXPERTS * CAP + 2 * TM, PACKED), U32))
    scratch = [pltpu.VMEM((1, LANES), F32), pltpu.VMEM((TM // SUBLANES, SUBLANES, PACKED), U32),
               pltpu.VMEM((SUBLANES, TM), jnp.int32), pltpu.SMEM((SUBLANES, TM), jnp.int32),
               pltpu.SemaphoreType.DMA(()), pltpu.SemaphoreType.DMA(())]
    return in_specs, out_specs, out_shape, scratch


def _cond_row(i):
    return jnp.where(i < NP_TILES, DEC_BATCH, (i - NP_TILES) // S_TILES_PER_SEQ)


def _ctx_blk(i):
    return jnp.minimum(i, NP_TILES - 1)


def _lat_blk(i):
    return jnp.maximum(i - NP_TILES, 0)


def _even_out_kernel(xp_ref, xs_ref, o_ref, r_ref, wout_ref, *tail):
    y = jnp.dot(r_ref[...], wout_ref[H_MLA * MLA_V:, :], preferred_element_type=F32)
    for p in range(N_PAIR):
        y = y + jnp.dot(o_ref[p], wout_ref[p * LANES:(p + 1) * LANES, :], preferred_element_type=F32)
    x = jnp.where(pl.program_id(0) < NP_TILES, xp_ref[...], xs_ref[...])
    _residual_and_route(x, y, *tail)


def _even_out(xp, xs, o_mla, o_ret, wout, mod, gffn, wrh, wrl, br):
    row = lambda i: (i, 0)
    tail_in, out_specs, out_shape, scratch = _tail_specs()
    return pl.pallas_call(
        _even_out_kernel,
        grid=(NT_TILES,),
        in_specs=[pl.BlockSpec((TM, D_MODEL), lambda i: (_ctx_blk(i), 0)),
                  pl.BlockSpec((TM, D_MODEL), lambda i: (_lat_blk(i), 0)),
                  pl.BlockSpec((N_PAIR, TM, LANES), lambda i: (0, i, 0)),
                  pl.BlockSpec((TM, H_RET * RET_DV), row),
                  _full(wout.shape)] + tail_in,
        out_specs=out_specs,
        out_shape=out_shape,
        scratch_shapes=scratch,
        compiler_params=_cparams(("arbitrary",)),
        name="even_out_route",
    )(xp, xs, o_mla, o_ret, wout, mod, gffn, wrh, wrl, br)


def _conf_in_kernel(x_ref, mod_ref, g_ref, win_ref, u_ref):
    h = _prenorm(x_ref[...], g_ref[...], mod_ref[0, 0:1, :], mod_ref[0, 1:2, :])
    z = jnp.dot(h.astype(BF16), win_ref[...], preferred_element_type=F32)
    u_ref[...] = z[:, :CONV_DIM] * jax.nn.sigmoid(z[:, CONV_DIM:])


def _conf_in(x, mod, gain, win):
    row = lambda i: (i, 0)
    return pl.pallas_call(
        _conf_in_kernel,
        grid=(NT_TILES,),
        in_specs=[pl.BlockSpec((TM, D_MODEL), row),
                  pl.BlockSpec((1, 6, D_MODEL), lambda i: (_cond_row(i), 0, 0)),
                  _full(gain.shape), _full(win.shape)],
        out_specs=pl.BlockSpec((TM, CONV_DIM), row),
        out_shape=jax.ShapeDtypeStruct((NT, CONV_DIM), F32),
        compiler_params=_cparams(("arbitrary",)),
        name="conformer_in_glu",
    )(x, mod, gain, win)


CONV_ROWS = 32
SHIFT_ROWS = TM + 2 * HALO - SUBLANES


def _conv_out_kernel(x_ref, u_ref, ul_ref, ur_ref, dw_ref, dwb_ref, lng_ref, lnb_ref, wout_ref, *tail_and_scratch):
    tail = tail_and_scratch[:-3]
    ext_scr, shift_scr, act_scr = tail_and_scratch[-3:]
    i = pl.program_id(0)
    t = (i - NP_TILES) % S_TILES_PER_SEQ
    first = (i < NP_TILES) | (t == 0)
    last = (i < NP_TILES) | (t == S_TILES_PER_SEQ - 1)
    ext_scr[0:HALO, :] = jnp.where(first, 0.0, ul_ref[...])
    ext_scr[HALO:HALO + TM, :] = u_ref[...]
    ext_scr[HALO + TM:, :] = jnp.where(last, 0.0, ur_ref[...])
    for s in range(SUBLANES):
        shift_scr[s] = ext_scr[pl.ds(s, SHIFT_ROWS), :]
    xbuf_ref, stage_scr, slot_vm, slot_sm, row_sem, slot_sem = tail[8], *tail[10:15]

    @pl.when(i == 0)
    def _():
        _prime_late_issue(xbuf_ref, stage_scr, slot_vm, slot_sm, slot_sem)

    off = HALO - CONV_W // 2
    rows_per_block = CONV_ROWS
    for rb in range(TM // CONV_ROWS):
        acc = jnp.zeros((CONV_ROWS, CONV_DIM), F32)
        for kk in range(CONV_W):
            s, a = (off + kk) % SUBLANES, (off + kk) // SUBLANES * SUBLANES
            acc = acc + shift_scr[s, pl.ds(rb * CONV_ROWS + a, CONV_ROWS), :] * dw_ref[kk:kk + 1, :]
            for r in range(rb * rows_per_block + kk * rows_per_block // CONV_W,
                           rb * rows_per_block + (kk + 1) * rows_per_block // CONV_W):
                _issue_row(xbuf_ref, stage_scr, slot_sm, row_sem, r // SUBLANES, r % SUBLANES)
        c = acc + dwb_ref[...]
        mu = jnp.mean(c, axis=-1, keepdims=True)
        var = jnp.mean(jnp.square(c - mu), axis=-1, keepdims=True)
        c = (c - mu) * lax.rsqrt(var + EPS) * lng_ref[...] + lnb_ref[...]
        act_scr[rb * CONV_ROWS:(rb + 1) * CONV_ROWS, :] = _silu(c).astype(BF16)
    y = jnp.dot(act_scr[...], wout_ref[...], preferred_element_type=F32)
    _residual_and_route(x_ref[...], y, *tail, late_issue=True)


def _conv_out(x, u, dw, dwb, lng, lnb, wout, mod, gffn, wrh, wrl, br):
    row = lambda i: (i, 0)
    per = TM // HALO
    n_halo = NT // HALO
    tail_in, out_specs, out_shape, scratch = _tail_specs()
    return pl.pallas_call(
        _conv_out_kernel,
        grid=(NT_TILES,),
        in_specs=[pl.BlockSpec((TM, D_MODEL), row),
                  pl.BlockSpec((TM, CONV_DIM), row),
                  pl.BlockSpec((HALO, CONV_DIM), lambda i: (jnp.maximum(i * per - 1, 0), 0)),
                  pl.BlockSpec((HALO, CONV_DIM), lambda i: (jnp.minimum((i + 1) * per, n_halo - 1), 0)),
                  _full(dw.shape), _full(dwb.shape), _full(lng.shape), _full(lnb.shape), _full(wout.shape)] + tail_in,
        out_specs=out_specs,
        out_shape=out_shape,
        scratch_shapes=scratch + [pltpu.VMEM((TM + 2 * HALO, CONV_DIM), F32),
                                  pltpu.VMEM((SUBLANES, SHIFT_ROWS, CONV_DIM), F32),
                                  pltpu.VMEM((TM, CONV_DIM), BF16)],
        compiler_params=_cparams(("arbitrary",)),
        name="conformer_conv_out_route",
    )(x, u, u, u, dw, dwb, lng, lnb, wout, mod, gffn, wrh, wrl, br)


def _pad_fill_kernel(cnt_ref, xin_ref, xbuf_ref, zero_scr, sem):
    del xin_ref
    e = pl.program_id(0)
    n = cnt_ref[e]
    pad = (EB - n % EB) % EB
    zero_scr[...] = jnp.zeros_like(zero_scr)

    def start(r, c):
        _row_copy(zero_scr, 0, xbuf_ref, e * CAP + n + r, sem).start()
        return c

    def wait(r, c):
        _row_copy(zero_scr, 0, xbuf_ref, e * CAP + n + r, sem).wait()
        return c

    lax.fori_loop(0, pad, start, 0)
    lax.fori_loop(0, pad, wait, 0)


def _pad_fill(counts, xbuf):
    return pl.pallas_call(
        _pad_fill_kernel,
        grid_spec=pltpu.PrefetchScalarGridSpec(
            num_scalar_prefetch=1,
            grid=(N_EXPERTS,),
            in_specs=[pl.BlockSpec(memory_space=pl.ANY)],
            out_specs=pl.BlockSpec(memory_space=pl.ANY),
            scratch_shapes=[pltpu.VMEM((SUBLANES, PACKED), U32), pltpu.SemaphoreType.DMA(())]),
        out_shape=jax.ShapeDtypeStruct(xbuf.shape, xbuf.dtype),
        input_output_aliases={1: 0},
        compiler_params=_cparams(("arbitrary",)),
        name="moe_pad_fill",
    )(counts, xbuf)


def _expert_kernel(blk_ref, exp_ref, flag_ref, x_ref, wu_ref, wd_ref, o_ref, wu_scr, wd_scr):
    i = pl.program_id(0)
    flag = flag_ref[i]

    @pl.when((flag & 2) != 0)
    def _():
        wu_scr[...] = wu_ref[0, 0].astype(BF16)
        wd_scr[...] = wd_ref[0, 0].astype(BF16)

    @pl.when((flag & 1) != 0)
    def _():
        w = x_ref[...]
        xa = lax.bitcast_convert_type(w & jnp.uint32(0xFFFF0000), F32).astype(BF16)
        xb = lax.bitcast_convert_type(w << 16, F32).astype(BF16)
        ab = (jnp.dot(xa, wu_scr[:PACKED, :], preferred_element_type=F32)
              + jnp.dot(xb, wu_scr[PACKED:, :], preferred_element_type=F32))
        mid = (_silu(ab[:, :D_EXPERT]) * ab[:, D_EXPERT:]).astype(BF16)
        o_ref[...] = jnp.dot(mid, wd_scr[...], preferred_element_type=F32)


def _experts(blk, exp, flag, xbuf, w_up, w_down, layer):
    return pl.pallas_call(
        _expert_kernel,
        grid_spec=pltpu.PrefetchScalarGridSpec(
            num_scalar_prefetch=3,
            grid=(N_ITEMS,),
            in_specs=[pl.BlockSpec((EB, PACKED), lambda i, b, e, f: (b[i], 0)),
                      pl.BlockSpec((1, 1, D_MODEL, 2 * D_EXPERT), lambda i, b, e, f: (layer, e[i], 0, 0)),
                      pl.BlockSpec((1, 1, D_EXPERT, D_MODEL), lambda i, b, e, f: (layer, e[i], 0, 0))],
            out_specs=pl.BlockSpec((EB, D_MODEL), lambda i, b, e, f: (b[i], 0)),
            scratch_shapes=[pltpu.VMEM((D_MODEL, 2 * D_EXPERT), BF16), pltpu.VMEM((D_EXPERT, D_MODEL), BF16)]),
        out_shape=jax.ShapeDtypeStruct((N_EXPERTS * CAP, D_MODEL), F32),
        compiler_params=_cparams(("arbitrary",)),
        name="moe_experts",
    )(blk, exp, flag, xbuf, w_up, w_down)


def _combine_kernel(slot_ref, x_ref, route_ref, mod_ref, ybuf_ref, *rest, conformer_next):
    g_scr, sem = rest[-2:]
    if conformer_next:
        modn_ref, gain_ref, win_ref = rest[:3]
        outs = rest[3:-2]
    else:
        outs = rest[:-2]
    i = pl.program_id(0)
    n_steps = pl.num_programs(0)

    def gather(tile, buf):
        base = tile * (2 * TM)
        for r in range(TM):
            _row_copy(ybuf_ref, slot_ref[base + 2 * r], g_scr.at[buf], r, sem.at[buf]).start(priority=0)
            _row_copy(ybuf_ref, slot_ref[base + 2 * r + 1], g_scr.at[buf], TM + r, sem.at[buf]).start(priority=1)

    cur = i % 2

    @pl.when(i == 0)
    def _():
        gather(0, 0)

    for parity in range(2):
        @pl.when((i + 1 < n_steps) & (cur == parity))
        def _():
            gather(i + 1, 1 - parity)

    _wait_rows(ybuf_ref, g_scr.at[cur], sem.at[cur], 2 * TM)
    route = route_ref[...]
    y = route[:, 2:3] * g_scr[cur, 0:TM, :] + route[:, 3:4] * g_scr[cur, TM:2 * TM, :]
    res = x_ref[...] + mod_ref[0, 5:6, :] * y
    if conformer_next:
        outs[0][...] = res
        h = _prenorm(res, gain_ref[...], modn_ref[0, 0:1, :], modn_ref[0, 1:2, :])
        z = jnp.dot(h.astype(BF16), win_ref[...], preferred_element_type=F32)
        outs[1][...] = z[:, :CONV_DIM] * jax.nn.sigmoid(z[:, CONV_DIM:])
    elif len(outs) == 1:
        outs[0][...] = res
    else:
        @pl.when(i < NP_TILES)
        def _():
            outs[0][...] = res

        @pl.when(i >= NP_TILES)
        def _():
            outs[1][...] = res


def _combine(slots, x1, route, mod, ybuf, *, split, conformer_next=None):
    row = lambda i, s: (i, 0)
    extra_in, extra_args = [], []
    if conformer_next is not None:
        mod_n, gain_n, win_n = conformer_next
        extra_in = [pl.BlockSpec((1, 6, D_MODEL), lambda i, s: (_cond_row(i), 0, 0)), _full(gain_n.shape),
                    _full(win_n.shape)]
        extra_args = [mod_n, gain_n, win_n]
        out_specs = (pl.BlockSpec((TM, D_MODEL), row), pl.BlockSpec((TM, CONV_DIM), row))
        out_shape = (jax.ShapeDtypeStruct((NT, D_MODEL), F32), jax.ShapeDtypeStruct((NT, CONV_DIM), F32))
    elif split:
        out_specs = (pl.BlockSpec((TM, D_MODEL), lambda i, s: (_ctx_blk(i), 0)),
                     pl.BlockSpec((TM, D_MODEL), lambda i, s: (_lat_blk(i), 0)))
        out_shape = (jax.ShapeDtypeStruct((NP, D_MODEL), F32), jax.ShapeDtypeStruct((NS, D_MODEL), F32))
    else:
        out_specs = pl.BlockSpec((TM, D_MODEL), row)
        out_shape = jax.ShapeDtypeStruct((NT, D_MODEL), F32)
    return pl.pallas_call(
        functools.partial(_combine_kernel, conformer_next=conformer_next is not None),
        grid_spec=pltpu.PrefetchScalarGridSpec(
            num_scalar_prefetch=1,
            grid=(NT_TILES,),
            in_specs=[pl.BlockSpec((TM, D_MODEL), row),
                      pl.BlockSpec((TM, LANES), row),
                      pl.BlockSpec((1, 6, D_MODEL), lambda i, s: (_cond_row(i), 0, 0)),
                      pl.BlockSpec(memory_space=pl.ANY)] + extra_in,
            out_specs=out_specs,
            scratch_shapes=[pltpu.VMEM((2, 2 * TM, D_MODEL), F32), pltpu.SemaphoreType.DMA((2,))]),
        out_shape=out_shape,
        compiler_params=_cparams(("arbitrary",)),
        name="moe_combine",
    )(slots, x1, route, mod, ybuf, *extra_args)


def _moe(x1, xbuf, route, counts, mod, w_up, w_down, layer, *, split, conformer_next=None):
    slots = route[:, 4:6].astype(jnp.int32).reshape(-1)
    cnt = counts[0, :N_EXPERTS].astype(jnp.int32)
    nblk = (cnt + EB - 1) // EB
    ends = jnp.cumsum(nblk)
    total = ends[-1]
    item = jnp.arange(N_ITEMS, dtype=jnp.int32)
    valid = item < total
    item_c = jnp.minimum(item, total - 1)
    exp = jnp.minimum(jnp.sum((item_c[:, None] >= ends[None, :]).astype(jnp.int32), axis=1), N_EXPERTS - 1)
    j = item_c - (ends[exp] - nblk[exp])
    blk = (exp * CAP_BLOCKS + j).astype(jnp.int32)
    flag = (valid.astype(jnp.int32) + 2 * (valid & (j == 0)).astype(jnp.int32)).astype(jnp.int32)

    xbuf = _pad_fill(cnt, xbuf)
    ybuf = _experts(blk, exp, flag, xbuf, w_up, w_down, layer)
    return _combine(slots, x1, route, mod, ybuf, split=split, conformer_next=conformer_next)


def _pad_heads(w, width, real):
    lead = w.shape[:-1]
    w = w.reshape(lead + (H_MLA, real))
    w = jnp.pad(w, [(0, 0)] * len(lead) + [(0, 0), (0, width - real)])
    return w.reshape(lead + (H_MLA * width,))


def _rope_tables():
    L = DEC_SEQ
    rows = L // GRID_W
    r = jnp.repeat(jnp.arange(rows, dtype=F32), GRID_W)
    col = jnp.tile(jnp.arange(GRID_W, dtype=F32), rows)
    n_f = ROPE_DIM // 4
    freqs = ROPE_BASE ** (-jnp.arange(n_f, dtype=F32) / n_f)
    ang = jnp.concatenate([r[:, None] * freqs, col[:, None] * freqs], axis=-1)
    cos, sin = jnp.cos(ang), jnp.sin(ang)
    half = ROPE_DIM // 2
    z = lambda n: jnp.zeros((L, n), F32)
    o = lambda n: jnp.ones((L, n), F32)
    cos_t = jnp.concatenate([o(MLA_NOPE), cos, cos, o(LANES - MLA_QK)], axis=-1)
    sin_lo = jnp.concatenate([z(MLA_NOPE), -sin, z(half), z(LANES - MLA_QK)], axis=-1)
    sin_hi = jnp.concatenate([z(MLA_NOPE), z(half), sin, z(LANES - MLA_QK)], axis=-1)
    ident = (jnp.ones((TM, LANES), F32), jnp.zeros((TM, LANES), F32), jnp.zeros((TM, LANES), F32))
    return (cos_t, sin_lo, sin_hi), ident


def _even_weights(w_in, q_a_norm, w_q_up, kv_a_norm, w_kv_up, q_norm, k_norm):
    kr_cols = jnp.pad(w_in[:, Q_RANK + KV_RANK:Q_RANK + KV_RANK + ROPE_DIM], ((0, 0), (MLA_NOPE, LANES - MLA_QK)))
    rest = w_in[:, Q_RANK + KV_RANK + ROPE_DIM:]
    win = jnp.concatenate([w_in[:, :Q_RANK + KV_RANK], kr_cols, rest], axis=-1).astype(BF16)
    wq = _pad_heads(w_q_up, LANES, MLA_QK).astype(BF16)
    kv = w_kv_up.reshape(KV_RANK, H_MLA, MLA_NOPE + MLA_V)
    wk = _pad_heads(kv[:, :, :MLA_NOPE].reshape(KV_RANK, H_MLA * MLA_NOPE), LANES, MLA_NOPE).astype(BF16)
    wv = kv[:, :, MLA_NOPE:].reshape(KV_RANK, H_MLA * MLA_V).astype(BF16)
    padg = lambda g: jnp.pad(g, (0, LANES - MLA_QK)).reshape(1, LANES)
    qg = padg(q_norm) * (MLA_QK ** -0.5)
    kg = padg(k_norm)
    return (win, q_a_norm.reshape(1, -1), wq, kv_a_norm.reshape(1, -1), wk, wv, qg, kg)


def _router_weights(w_group, b_group, w_expert, b_expert):
    fill = LANES - N_EXPERTS - N_GROUPS
    w = jnp.pad(jnp.concatenate([w_expert, w_group], axis=1), ((0, 0), (0, fill)))
    b = jnp.pad(jnp.concatenate([b_expert, b_group]), (0, fill)).reshape(1, LANES)
    hi = w.astype(BF16)
    lo = (w - hi.astype(F32)).astype(BF16)
    return hi, lo, b


def kernel(x_prompt, x_sample, cache_mla_ckv, cache_mla_krope, state_retention, c, c_ctx, ada_w, ada_b, norm_mix, norm_ffn, ev_w_in, ev_q_a_norm, ev_w_q_up, ev_kv_a_norm, ev_w_kv_up, ev_q_norm, ev_k_norm, ev_ret_decay, ev_ret_gn, ev_w_out, od_w_in, od_dw, od_dw_b, od_ln_g, od_ln_b, od_w_out, moe_w_group, moe_b_group, moe_w_expert, moe_b_expert, moe_w_up, moe_w_down):
    depth = ada_w.shape[0]
    x, xp, xs = None, x_prompt.reshape(NP, D_MODEL), x_sample.reshape(NS, D_MODEL)
    cond = jnp.concatenate([c, c_ctx[None, :], jnp.zeros((2 * SUBLANES - DEC_BATCH - 1, D_MODEL), F32)], axis=0)
    mods = _ada(cond, ada_w, ada_b).reshape(depth, cond.shape[0], 6, D_MODEL)
    rope, rope_ident = _rope_tables()
    new_ckv, new_krope, new_state = [], [], []
    u = None

    for l in range(depth):
        mod = mods[l]
        jj = l // 2
        router = _router_weights(moe_w_group[l], moe_b_group[l], moe_w_expert[l], moe_b_expert[l])
        gmix = norm_mix[l].reshape(1, D_MODEL)
        gffn = norm_ffn[l].reshape(1, D_MODEL)
        if l % 2 == 0:
            wts = _even_weights(ev_w_in[jj], ev_q_a_norm[jj], ev_w_q_up[jj], ev_kv_a_norm[jj], ev_w_kv_up[jj],
                                ev_q_norm[jj], ev_k_norm[jj])
            if xp is None:
                xp, xs = x[:NP], x[NP:]
            qp, kp, vp, ckv_p, kr_p, rq_p, rk_p, rv_p, rg_p = _even_in(
                xp, mod, gmix, wts, rope_ident, n_tiles=NP_TILES, x_blk=lambda i: i, mod_row=lambda i: DEC_BATCH,
                rope_blk=lambda i: 0, kv_rows=NP, kv_blk=lambda i: i)
            qs, ks, vs, _, _, rq_s, rk_s, rv_s, rg_s = _even_in(
                xs, mod, gmix, wts, rope, n_tiles=NS // TM, x_blk=lambda i: i,
                mod_row=lambda i: i // S_TILES_PER_SEQ, rope_blk=lambda i: i % S_TILES_PER_SEQ,
                kv_rows=DEC_BATCH * LK_S,
                kv_blk=lambda i: (i // S_TILES_PER_SEQ) * KV_BLOCKS_PER_SEQ + 1 + i % S_TILES_PER_SEQ)
            kr_cache = jnp.pad(cache_mla_krope[:, jj].reshape(DEC_BATCH * PAST_LEN, ROPE_DIM),
                               ((0, 0), (MLA_NOPE, LANES - MLA_QK)))
            ks, vs = _ctx_kv(cache_mla_ckv[:, jj].reshape(DEC_BATCH * PAST_LEN, KV_RANK), kr_cache,
                             wts[4], wts[5], wts[7], rope_ident, ks, vs)
            o_mla = _attention(qp, kp, vp, None, nb=BATCH, tq=SEQ, lk=SEQ, o_row0=0, pairs=N_PAIR)
            o_mla = _attention(qs, ks, vs, o_mla, nb=DEC_BATCH, tq=TQ_LATENT, lk=LK_S, o_row0=NP, pairs=N_PAIR)

            decay = ev_ret_decay[jj]
            gn = ev_ret_gn[jj].reshape(1, -1)
            zero_state = jnp.zeros((BATCH, H_RET, RET_DK, RET_DV), F32)
            ncp, ncs = SEQ // RET_STEP_ROWS, DEC_SEQ // RET_STEP_ROWS
            o_ret, sf_p, sb_p = _retention(decay, rq_p, rk_p, rv_p, zero_state, zero_state, rg_p, gn,
                                           nb=BATCH, nc=ncp, o_prev=None, o_blk0=0)
            o_ret, _, _ = _retention(decay, rq_s, rk_s, rv_s, state_retention[:, jj, 0], state_retention[:, jj, 1],
                                     rg_s, gn, nb=DEC_BATCH, nc=ncs, o_prev=o_ret, o_blk0=NP // RET_STEP_ROWS)
            new_ckv.append(ckv_p.reshape(BATCH, SEQ, KV_RANK))
            new_krope.append(kr_p[:, MLA_NOPE:MLA_QK].reshape(BATCH, SEQ, ROPE_DIM))
            new_state.append(jnp.stack([sf_p, sb_p], axis=1))
            x1, route, counts, xbuf = _even_out(xp, xs, o_mla, o_ret, ev_w_out[jj].astype(BF16), mod, gffn, *router)
        else:
            x = jnp.concatenate([xp, xs], axis=0) if x is None else x
            if u is None:
                u = _conf_in(x, mod, gmix, od_w_in[jj].astype(BF16))
            x1, route, counts, xbuf = _conv_out(
                x, u, od_dw[jj], od_dw_b[jj].reshape(1, -1), od_ln_g[jj].reshape(1, -1), od_ln_b[jj].reshape(1, -1),
                od_w_out[jj].astype(BF16), mod, gffn, *router)
        u = None
        if l == depth - 1:
            xp, xs = _moe(x1, xbuf, route, counts, mod, moe_w_up, moe_w_down, l, split=True)
            x = None
        elif (l + 1) % 2 == 1:
            nxt = (mods[l + 1], norm_mix[l + 1].reshape(1, D_MODEL), od_w_in[(l + 1) // 2].astype(BF16))
            x, u = _moe(x1, xbuf, route, counts, mod, moe_w_up, moe_w_down, l, split=False, conformer_next=nxt)
            xp, xs = None, None
        else:
            x = _moe(x1, xbuf, route, counts, mod, moe_w_up, moe_w_down, l, split=False)
            xp, xs = None, None

    return (xp.reshape(BATCH, SEQ, D_MODEL), xs.reshape(DEC_BATCH, DEC_SEQ, D_MODEL),
            jnp.stack(new_ckv, axis=1), jnp.stack(new_krope, axis=1), jnp.stack(new_state, axis=1))
```
